```python
import math
import jax, jax.numpy as jnp
from jax import lax
import numpy as np

D_MODEL = 1024
BATCH = 4
SEQ = 8192
DEPTH = 4
DEC_BATCH = 32
DEC_SEQ = 2048
PAST_LEN = 128

GRID_W = 64
N_HEADS = 8
ATTN_DIM = D_MODEL // 2
HEAD_DIM = ATTN_DIM // N_HEADS
WIN_ROWS = 8
WIN_COLS = 16
POOL_WINDOWS = (2, 4, 8, 16)
N_POOL_GROUPS = len(POOL_WINDOWS)
POOL_DIM = D_MODEL // 2
POOL_GROUP_DIM = POOL_DIM // N_POOL_GROUPS
PROJ_DIM = 3 * ATTN_DIM + POOL_DIM + 2 * D_MODEL
N_EXPERTS = 16
N_GROUPS = 4
EXPERTS_PER_GROUP = N_EXPERTS // N_GROUPS
TOP_K = 2
D_FF = 2 * D_MODEL
MOE_BLOCK = 256
ALPHA = (2 * DEPTH) ** 0.25
BETA = (8 * DEPTH) ** -0.25
LN_EPS = 1e-5

kernel_name = "hybrid_natten_pool_grouped_moe_encoder"


def layer_norm(x, g, b):
    xf = x.astype(jnp.float32)
    mu = jnp.mean(xf, axis=-1, keepdims=True)
    var = jnp.mean(jnp.square(xf - mu), axis=-1, keepdims=True)
    y = (xf - mu) * lax.rsqrt(var + LN_EPS) * g.astype(jnp.float32) + b.astype(jnp.float32)
    return y.astype(x.dtype)


def neighborhood_attention(q, k, v, rpb):
    B, L, H, Dh = q.shape
    rows = L // GRID_W
    kr = min(WIN_ROWS, rows)
    qg = q.reshape(B, rows, GRID_W, H, Dh)
    kg = k.reshape(B, rows, GRID_W, H, Dh)
    vg = v.reshape(B, rows, GRID_W, H, Dh)
    col = jnp.arange(GRID_W, dtype=jnp.int32)
    col_start = jnp.clip(col - WIN_COLS // 2, 0, GRID_W - WIN_COLS)
    col_idx = col_start[:, None] + jnp.arange(WIN_COLS, dtype=jnp.int32)[None, :]
    dc_idx = col_idx - col[:, None] + (WIN_COLS - 1)
    scale = HEAD_DIM ** -0.5

    def row_step(r):
        rs = jnp.clip(r - kr // 2, 0, rows - kr)
        k_rows = lax.dynamic_slice_in_dim(kg, rs, kr, axis=1)
        v_rows = lax.dynamic_slice_in_dim(vg, rs, kr, axis=1)
        k_win = jnp.take(k_rows, col_idx, axis=2)
        v_win = jnp.take(v_rows, col_idx, axis=2)
        q_row = lax.dynamic_index_in_dim(qg, r, axis=1, keepdims=False)
        dr_idx = rs + jnp.arange(kr, dtype=jnp.int32) - r + (WIN_ROWS - 1)
        bias = rpb[:, dr_idx[:, None, None], dc_idx[None, :, :]]
        bias = jnp.transpose(bias, (0, 2, 1, 3)).astype(jnp.float32)
        s = jnp.einsum('bqhd,biqjhd->bhqij', q_row, k_win).astype(jnp.float32) * scale + bias[None]
        p = jax.nn.softmax(s.reshape(B, H, GRID_W, kr * WIN_COLS), axis=-1)
        p = p.reshape(s.shape).astype(v.dtype)
        return jnp.einsum('bhqij,biqjhd->bqhd', p, v_win)

    out = lax.map(row_step, jnp.arange(rows, dtype=jnp.int32))
    return jnp.transpose(out, (1, 0, 2, 3, 4)).reshape(B, L, H * Dh)


def pool_mixer(p, w_pool, b_pool, pool_scale):
    B, L, _ = p.shape
    pg = p.reshape(B, L, N_POOL_GROUPS, POOL_GROUP_DIM)
    pos = jnp.arange(L, dtype=jnp.int32)
    diffs = []
    for g, w in enumerate(POOL_WINDOWS):
        xg = pg[:, :, g].astype(jnp.float32)
        cs = jnp.concatenate([jnp.zeros((B, 1, POOL_GROUP_DIM), jnp.float32),
                              jnp.cumsum(xg, axis=1)], axis=1)
        lo = jnp.clip(pos - w // 2, 0, L)
        hi = jnp.clip(pos + w // 2, 0, L)
        mean = (cs[:, hi] - cs[:, lo]) / (hi - lo).astype(jnp.float32)[None, :, None]
        diffs.append(mean - xg)
    d = jnp.stack(diffs, axis=2).astype(p.dtype)
    y = jnp.einsum('blgc,gcd->blgd', d, w_pool) + b_pool
    return y.reshape(B, L, POOL_DIM) * pool_scale


def mixer_block(x, w_in, b_in, rpb, w_pool, b_pool, pool_scale, w_oa, w_op, w_out, b_out):
    B, L, _ = x.shape
    proj = x @ w_in + b_in
    splits = [ATTN_DIM, 2 * ATTN_DIM, 3 * ATTN_DIM, 3 * ATTN_DIM + POOL_DIM,
              3 * ATTN_DIM + POOL_DIM + D_MODEL]
    q, k, v, p, ga, gp = jnp.split(proj, splits, axis=-1)
    to_heads = lambda t: t.reshape(B, L, N_HEADS, HEAD_DIM)
    a = neighborhood_attention(to_heads(q), to_heads(k), to_heads(v), rpb)
    pm = pool_mixer(p, w_pool, b_pool, pool_scale)
    mix = jax.nn.sigmoid(ga) * (a @ w_oa) + jax.nn.sigmoid(gp) * (pm @ w_op)
    return mix @ w_out + b_out


def moe_ffn(x, w_router, b_router, w1, b1, w2, b2):
    B, L, D = x.shape
    T = B * L
    xt = x.reshape(T, D)
    logits = xt.astype(jnp.float32) @ w_router.astype(jnp.float32) + b_router.astype(jnp.float32)
    probs = jax.nn.softmax(logits, axis=-1)
    probs_g = probs.reshape(T, N_GROUPS, EXPERTS_PER_GROUP)
    group_score = lax.top_k(probs_g, TOP_K)[0].sum(-1)
    chosen = jnp.argmax(group_score, axis=-1).astype(jnp.int32)
    in_group = jnp.take_along_axis(probs_g, chosen[:, None, None], axis=1)[:, 0]
    vals, idx = lax.top_k(in_group, TOP_K)
    expert = chosen[:, None] * EXPERTS_PER_GROUP + idx.astype(jnp.int32)
    gates = (vals / jnp.sum(vals, axis=-1, keepdims=True)).astype(x.dtype)

    A = T * TOP_K
    e_flat = expert.reshape(A)
    tok_flat = jnp.repeat(jnp.arange(T, dtype=jnp.int32), TOP_K)
    w_flat = gates.reshape(A)
    order = jnp.argsort(e_flat)
    e_sorted = e_flat[order]
    counts = jnp.bincount(e_flat, length=N_EXPERTS).astype(jnp.int32)
    starts = jnp.cumsum(counts) - counts
    padded = (counts + MOE_BLOCK - 1) // MOE_BLOCK * MOE_BLOCK
    pad_ends = jnp.cumsum(padded)
    pad_starts = pad_ends - padded
    dest = pad_starts[e_sorted] + jnp.arange(A, dtype=jnp.int32) - starts[e_sorted]
    n_blocks = -(-(A + N_EXPERTS * (MOE_BLOCK - 1)) // MOE_BLOCK)
    P = n_blocks * MOE_BLOCK
    buf_tok = jnp.full((P,), T, jnp.int32).at[dest].set(tok_flat[order])
    buf_w = jnp.zeros((P,), x.dtype).at[dest].set(w_flat[order])
    block_e = jnp.clip(jnp.searchsorted(pad_ends, jnp.arange(n_blocks, dtype=jnp.int32) * MOE_BLOCK,
                                        side='right'), 0, N_EXPERTS - 1).astype(jnp.int32)
    x_pad = jnp.concatenate([xt, jnp.zeros((1, D), xt.dtype)], axis=0)

    def block_ffn(args):
        tok, e = args
        xb = jnp.take(x_pad, tok, axis=0)
        hdn = jax.nn.gelu(xb @ w1[e] + b1[e], approximate=False)
        return hdn @ w2[e] + b2[e]

    y_buf = lax.map(block_ffn, (buf_tok.reshape(n_blocks, MOE_BLOCK), block_e))
    y = jnp.zeros((T + 1, D), x.dtype).at[buf_tok].add(y_buf.reshape(P, D) * buf_w[:, None])
    return y[:T].reshape(B, L, D)


def setup_inputs(seed: int = 0) -> dict:
    key = jax.random.key(seed)
    ks = jax.random.split(key, 24)
    nrm = lambda k, shape, s: jax.random.normal(k, shape, jnp.float32) * s
    col_scale = jnp.concatenate([jnp.ones((2 * ATTN_DIM,), jnp.float32),
                                 jnp.full((ATTN_DIM,), BETA, jnp.float32),
                                 jnp.ones((POOL_DIM + 2 * D_MODEL,), jnp.float32)])
    return {
        "x_prompt": nrm(ks[0], (BATCH, SEQ, D_MODEL), 1.0),
        "x_sample": nrm(ks[1], (DEC_BATCH, DEC_SEQ, D_MODEL), 1.0),
        "ln_in_g": 1.0 + nrm(ks[2], (D_MODEL,), 0.05),
        "ln_in_b": nrm(ks[3], (D_MODEL,), 0.02),
        "w_in": nrm(ks[4], (DEPTH, D_MODEL, PROJ_DIM), D_MODEL ** -0.5) * col_scale,
        "b_in": nrm(ks[5], (DEPTH, PROJ_DIM), 0.02),
        "rpb": nrm(ks[6], (DEPTH, N_HEADS, 2 * WIN_ROWS - 1, 2 * WIN_COLS - 1), 0.1),
        "w_pool": nrm(ks[7], (DEPTH, N_POOL_GROUPS, POOL_GROUP_DIM, POOL_GROUP_DIM), POOL_GROUP_DIM ** -0.5),
        "b_pool": nrm(ks[8], (DEPTH, N_POOL_GROUPS, POOL_GROUP_DIM), 0.02),
        "pool_scale": 1.0 + nrm(ks[9], (DEPTH, POOL_DIM), 0.1),
        "w_oa": nrm(ks[10], (DEPTH, ATTN_DIM, D_MODEL), BETA * ATTN_DIM ** -0.5),
        "w_op": nrm(ks[11], (DEPTH, POOL_DIM, D_MODEL), BETA * POOL_DIM ** -0.5),
        "w_out": nrm(ks[12], (DEPTH, D_MODEL, D_MODEL), BETA * D_MODEL ** -0.5),
        "b_out": nrm(ks[13], (DEPTH, D_MODEL), 0.02),
        "ln1_g": 1.0 + nrm(ks[14], (DEPTH, D_MODEL), 0.05),
        "ln1_b": nrm(ks[15], (DEPTH, D_MODEL), 0.02),
        "w_router": nrm(ks[16], (D_MODEL, N_EXPERTS), D_MODEL ** -0.5),
        "b_router": nrm(ks[17], (N_EXPERTS,), 0.01),
        "w1": nrm(ks[18], (DEPTH, N_EXPERTS, D_MODEL, D_FF), BETA * D_MODEL ** -0.5),
        "b1": nrm(ks[19], (DEPTH, N_EXPERTS, D_FF), 0.02),
        "w2": nrm(ks[20], (DEPTH, N_EXPERTS, D_FF, D_MODEL), BETA * D_FF ** -0.5),
        "b2": nrm(ks[21], (DEPTH, N_EXPERTS, D_MODEL), 0.02),
        "ln2_g": 1.0 + nrm(ks[22], (DEPTH, D_MODEL), 0.05),
        "ln2_b": nrm(ks[23], (DEPTH, D_MODEL), 0.02),
    }


def reference(x_prompt, x_sample, ln_in_g, ln_in_b, w_in, b_in, rpb, w_pool, b_pool, pool_scale,
              w_oa, w_op, w_out, b_out, ln1_g, ln1_b, w_router, b_router, w1, b1, w2, b2,
              ln2_g, ln2_b):
    def run(x):
        x = layer_norm(x, ln_in_g, ln_in_b)
        for l in range(DEPTH):
            y = mixer_block(x, w_in[l], b_in[l], rpb[l], w_pool[l], b_pool[l], pool_scale[l],
                            w_oa[l], w_op[l], w_out[l], b_out[l])
            x = layer_norm(ALPHA * x + y, ln1_g[l], ln1_b[l])
            y = moe_ffn(x, w_router, b_router, w1[l], b1[l], w2[l], b2[l])
            x = layer_norm(ALPHA * x + y, ln2_g[l], ln2_b[l])
        return x

    y_prompt = run(x_prompt)
    y_sample = run(x_sample)
    return (y_prompt, y_sample)
```

```python
import functools

import jax
import jax.numpy as jnp
from jax import lax
from jax.experimental import pallas as pl
from jax.experimental.pallas import tpu as pltpu

GRID_W = 64
N_HEADS = 8
HEAD_DIM = 64
WIN_ROWS = 8
WIN_COLS = 16
POOL_WINDOWS = (2, 4, 8, 16)
N_EXPERTS = 16
N_GROUPS = 4
EXPERTS_PER_GROUP = 4
PAIRS = ((0, 1), (0, 2), (0, 3), (1, 2), (1, 3), (2, 3))
N_CLASSES = N_GROUPS * len(PAIRS)
CLASS_ROWS = 32
MOE_BLOCK = 256
LN_EPS = 1e-5
NEG = -1e30

Q_ROWS = 8
KV_ROWS = 16
KV_BLK_ROWS = 4
TOK_TILE = 512
POOL_HALO = 8
VMEM_LIMIT = 56 * 1024 * 1024


def _layer_norm(x, g, b):
    mu = jnp.mean(x, axis=-1, keepdims=True)
    xc = x - mu
    var = jnp.mean(xc * xc, axis=-1, keepdims=True)
    return xc * lax.rsqrt(var + LN_EPS) * g + b


def _params(semantics):
    return pltpu.CompilerParams(dimension_semantics=semantics, vmem_limit_bytes=VMEM_LIMIT)


def _const_spec(shape):
    nd = len(shape)
    return pl.BlockSpec(shape, lambda *_: (0,) * nd)


def _ln_kernel(x_ref, g_ref, b_ref, o_ref):
    o_ref[...] = _layer_norm(x_ref[...], g_ref[...], b_ref[...])


def _ln_call(x, g, b):
    t, d = x.shape
    tm = TOK_TILE
    return pl.pallas_call(
        _ln_kernel,
        grid=(t // tm,),
        in_specs=[pl.BlockSpec((tm, d), lambda i: (i, 0)), _const_spec((1, d)), _const_spec((1, d))],
        out_specs=pl.BlockSpec((tm, d), lambda i: (i, 0)),
        out_shape=jax.ShapeDtypeStruct((t, d), jnp.float32),
        compiler_params=_params(("parallel",)),
        name="ln_in",
    )(x, g, b)


def _proj_kernel(x_ref, w_ref, b_ref, q_ref, k_ref, v_ref, p_ref, *, attn_dim, scale):
    acc = jnp.dot(x_ref[...].astype(jnp.bfloat16), w_ref[...], preferred_element_type=jnp.float32)
    acc = acc + b_ref[...]
    q_ref[...] = (acc[:, :attn_dim] * scale).astype(jnp.bfloat16)
    k_ref[...] = acc[:, attn_dim:2 * attn_dim].astype(jnp.bfloat16)
    v_ref[...] = acc[:, 2 * attn_dim:3 * attn_dim].astype(jnp.bfloat16)
    p_ref[...] = acc[:, 3 * attn_dim:]


def _proj_call(x, w, b, attn_dim, pool_dim):
    t, d = x.shape
    n = w.shape[1]
    tm = TOK_TILE
    row = lambda i: (i, 0)
    return pl.pallas_call(
        functools.partial(_proj_kernel, attn_dim=attn_dim, scale=HEAD_DIM ** -0.5),
        grid=(t // tm,),
        in_specs=[pl.BlockSpec((tm, d), row), _const_spec((d, n)), _const_spec((1, n))],
        out_specs=[pl.BlockSpec((tm, attn_dim), row)] * 3 + [pl.BlockSpec((tm, pool_dim), row)],
        out_shape=[jax.ShapeDtypeStruct((t, attn_dim), jnp.bfloat16)] * 3
        + [jax.ShapeDtypeStruct((t, pool_dim), jnp.float32)],
        compiler_params=_params(("parallel",)),
        name="proj",
    )(x, w, b)


def _attn_kernel(q_ref, k0, k1, k2, k3, v0, v1, v2, v3, tab_ref, o_ref, *, rows):
    nq = Q_ROWS * GRID_W
    nk = KV_ROWS * GRID_W
    r0 = pl.program_id(1) * Q_ROWS
    q_row = lax.broadcasted_iota(jnp.int32, (nq, 1), 0) // GRID_W + r0
    k_row = lax.broadcasted_iota(jnp.int32, (1, nk), 1) // GRID_W + (r0 - (KV_ROWS - Q_ROWS) // 2)
    row_start = jnp.clip(q_row - WIN_ROWS // 2, 0, rows - WIN_ROWS)
    row_valid = (k_row >= row_start) & (k_row < row_start + WIN_ROWS)

    q = q_ref[...]
    k = jnp.concatenate([k0[...], k1[...], k2[...], k3[...]], axis=0)
    v = jnp.concatenate([v0[...], v1[...], v2[...], v3[...]], axis=0)
    for h in range(N_HEADS):
        lanes = slice(h * HEAD_DIM, (h + 1) * HEAD_DIM)
        s = lax.dot_general(q[:, lanes], k[:, lanes], (((1,), (1,)), ((), ())),
                            preferred_element_type=jnp.float32)
        s = jnp.where(row_valid, s + tab_ref[h], NEG)
        m = jnp.max(s, axis=-1, keepdims=True)
        e = jnp.exp(s - m)
        denom = jnp.sum(e, axis=-1, keepdims=True)
        o = jnp.dot(e.astype(jnp.bfloat16), v[:, lanes], preferred_element_type=jnp.float32)
        o_ref[:, lanes] = (o / denom).astype(o_ref.dtype)


def _attn_call(q, k, v, table, batch, seq):
    attn_dim = q.shape[1]
    rows = seq // GRID_W
    assert rows % Q_ROWS == 0 and rows >= WIN_ROWS
    nq = Q_ROWS * GRID_W
    kb = KV_BLK_ROWS * GRID_W
    q_blocks = seq // nq
    kv_blocks = seq // kb
    lead = (KV_ROWS - Q_ROWS) // 2 // KV_BLK_ROWS

    def kv_spec(j):
        def index(b, i):
            blk = jnp.clip(i * (Q_ROWS // KV_BLK_ROWS) - lead + j, 0, kv_blocks - 1)
            return (b * kv_blocks + blk, 0)
        return pl.BlockSpec((kb, attn_dim), index)

    kv_specs = [kv_spec(j) for j in range(KV_ROWS // KV_BLK_ROWS)]
    return pl.pallas_call(
        functools.partial(_attn_kernel, rows=rows),
        grid=(batch, q_blocks),
        in_specs=[pl.BlockSpec((nq, attn_dim), lambda b, i: (b * q_blocks + i, 0))]
        + kv_specs + kv_specs + [_const_spec(table.shape)],
        out_specs=pl.BlockSpec((nq, attn_dim), lambda b, i: (b * q_blocks + i, 0)),
        out_shape=jax.ShapeDtypeStruct(q.shape, jnp.bfloat16),
        compiler_params=_params(("parallel", "parallel")),
        name="attn",
    )(q, k, k, k, k, v, v, v, v, table)


def _bias_table(rpb):
    i = jnp.arange(Q_ROWS)
    j = jnp.arange(KV_ROWS)
    c = jnp.arange(GRID_W)
    dr = j[None, :] - (KV_ROWS - Q_ROWS) // 2 - i[:, None] + (WIN_ROWS - 1)
    row_ok = (dr >= 0) & (dr <= 2 * WIN_ROWS - 2)
    col_start = jnp.clip(c - WIN_COLS // 2, 0, GRID_W - WIN_COLS)
    col_ok = (c[None, :] >= col_start[:, None]) & (c[None, :] < col_start[:, None] + WIN_COLS)
    dc = c[None, :] - c[:, None] + (WIN_COLS - 1)
    dr_c = jnp.clip(dr, 0, 2 * WIN_ROWS - 2)
    dc_c = jnp.clip(dc, 0, 2 * WIN_COLS - 2)
    tab = rpb[:, dr_c[:, None, :, None], dc_c[None, :, None, :]]
    ok = row_ok[:, None, :, None] & col_ok[None, :, None, :]
    tab = jnp.where(ok[None], tab.astype(jnp.float32), NEG)
    return tab.reshape(rpb.shape[0], Q_ROWS * GRID_W, KV_ROWS * GRID_W)


def _route_class(logits_t):
    m = jnp.max(logits_t, axis=0, keepdims=True)
    e = jnp.exp(logits_t - m)
    probs = e / jnp.sum(e, axis=0, keepdims=True)
    p = [probs[r:r + 1, :] for r in range(N_EXPERTS)]

    def top2_sum(a, b, c, d):
        hi1, lo1 = jnp.maximum(a, b), jnp.minimum(a, b)
        hi2, lo2 = jnp.maximum(c, d), jnp.minimum(c, d)
        return jnp.maximum(hi1, hi2) + jnp.maximum(jnp.minimum(hi1, hi2), jnp.maximum(lo1, lo2))

    score = [top2_sum(*p[4 * g:4 * g + 4]) for g in range(N_GROUPS)]
    best_g = jnp.zeros_like(score[0], dtype=jnp.int32)
    best_s = score[0]
    for g in range(1, N_GROUPS):
        upd = score[g] > best_s
        best_g = jnp.where(upd, g, best_g)
        best_s = jnp.where(upd, score[g], best_s)

    vals = []
    for r in range(EXPERTS_PER_GROUP):
        val = p[r]
        for g in range(1, N_GROUPS):
            val = jnp.where(best_g == g, p[4 * g + r], val)
        vals.append(val)

    def argmax4(vs):
        idx = jnp.zeros_like(best_g)
        top = vs[0]
        for r in range(1, EXPERTS_PER_GROUP):
            upd = vs[r] > top
            idx = jnp.where(upd, r, idx)
            top = jnp.where(upd, vs[r], top)
        return idx

    i1 = argmax4(vals)
    i2 = argmax4([jnp.where(i1 == r, -1.0, vals[r]) for r in range(EXPERTS_PER_GROUP)])
    lo = jnp.minimum(i1, i2)
    hi = jnp.maximum(i1, i2)
    pair = jnp.zeros_like(lo)
    for n, (a, b) in enumerate(PAIRS):
        pair = jnp.where((lo == a) & (hi == b), n, pair)
    return best_g * len(PAIRS) + pair


def _mix_kernel(x_ref, a_ref, p_ref, pprev_ref, pnext_ref, wg_ref, bg_ref, wpool_ref, bpool_ref,
                pscale_ref, woa_ref, wop_ref, wout_ref, bout_ref, lng_ref, lnb_ref, wr_ref, br_ref,
                x1_ref, cls_ref, halo_ref, *, seq, alpha):
    tm, d = x_ref.shape
    pool_dim = p_ref.shape[1]
    group_dim = pool_dim // len(POOL_WINDOWS)
    tiles_per_seq = seq // tm
    i = pl.program_id(0)
    tile_in_seq = i % tiles_per_seq
    at_start = tile_in_seq == 0
    at_end = tile_in_seq == tiles_per_seq - 1

    x = x_ref[...]
    xb = x.astype(jnp.bfloat16)
    gates = jax.nn.sigmoid(jnp.dot(xb, wg_ref[...], preferred_element_type=jnp.float32) + bg_ref[...])

    halo_ref[0:POOL_HALO, :] = jnp.where(at_start, 0.0, pprev_ref[...])
    halo_ref[POOL_HALO:POOL_HALO + tm, :] = p_ref[...]
    halo_ref[POOL_HALO + tm:, :] = jnp.where(at_end, 0.0, pnext_ref[...])
    pos = tile_in_seq * tm + lax.broadcasted_iota(jnp.int32, (tm, 1), 0)
    pooled = []
    for g, w in enumerate(POOL_WINDOWS):
        lanes = slice(g * group_dim, (g + 1) * group_dim)
        total = halo_ref[POOL_HALO - w // 2:POOL_HALO - w // 2 + tm, lanes]
        for off in range(-w // 2 + 1, w // 2):
            total = total + halo_ref[POOL_HALO + off:POOL_HALO + off + tm, lanes]
        count = jnp.clip(pos + w // 2, 0, seq) - jnp.clip(pos - w // 2, 0, seq)
        diff = total / count.astype(jnp.float32) - p_ref[:, lanes]
        y = jnp.dot(diff.astype(jnp.bfloat16), wpool_ref[g], preferred_element_type=jnp.float32)
        pooled.append((y + bpool_ref[:, lanes]) * pscale_ref[:, lanes])
    pm = jnp.concatenate(pooled, axis=1)

    attn_out = jnp.dot(a_ref[...], woa_ref[...], preferred_element_type=jnp.float32)
    pool_out = jnp.dot(pm.astype(jnp.bfloat16), wop_ref[...], preferred_element_type=jnp.float32)
    mix = gates[:, :d] * attn_out + gates[:, d:] * pool_out
    y = jnp.dot(mix.astype(jnp.bfloat16), wout_ref[...], preferred_element_type=jnp.float32) + bout_ref[...]
    x1 = _layer_norm(alpha * x + y, lng_ref[...], lnb_ref[...])
    x1_ref[...] = x1

    x_hi = x1.astype(jnp.bfloat16)
    x_lo = (x1 - x_hi.astype(jnp.float32)).astype(jnp.bfloat16)
    nt = (((1,), (1,)), ((), ()))
    both = lax.dot_general(wr_ref[...], x_hi, nt, preferred_element_type=jnp.float32)
    cross = lax.dot_general(wr_ref[0:N_EXPERTS, :], x_lo, nt, preferred_element_type=jnp.float32)
    logits_t = both[0:N_EXPERTS, :] + both[N_EXPERTS:, :] + cross + br_ref[...]
    cls_ref[0] = _route_class(logits_t)


def _mix_call(x, a, p, lw, seq, alpha):
    t, d = x.shape
    tm = TOK_TILE
    assert seq % tm == 0 and tm % POOL_HALO == 0
    attn_dim = a.shape[1]
    pool_dim = p.shape[1]
    row = lambda i: (i, 0)
    halo_blocks = t // POOL_HALO
    per_tile = tm // POOL_HALO
    prev = lambda i: (jnp.maximum(i * per_tile - 1, 0), 0)
    nxt = lambda i: (jnp.minimum((i + 1) * per_tile, halo_blocks - 1), 0)
    weights = [lw["w_gate"], lw["b_gate"], lw["w_pool"], lw["b_pool"], lw["pool_scale"], lw["w_oa"],
               lw["w_op"], lw["w_out"], lw["b_out"], lw["ln1_g"], lw["ln1_b"], lw["w_router_t"],
               lw["b_router_t"]]
    return pl.pallas_call(
        functools.partial(_mix_kernel, seq=seq, alpha=alpha),
        grid=(t // tm,),
        in_specs=[pl.BlockSpec((tm, d), row), pl.BlockSpec((tm, attn_dim), row),
                  pl.BlockSpec((tm, pool_dim), row), pl.BlockSpec((POOL_HALO, pool_dim), prev),
                  pl.BlockSpec((POOL_HALO, pool_dim), nxt)] + [_const_spec(w.shape) for w in weights],
        out_specs=[pl.BlockSpec((tm, d), row), pl.BlockSpec((1, 1, tm), lambda i: (i, 0, 0))],
        out_shape=[jax.ShapeDtypeStruct((t, d), jnp.float32),
                   jax.ShapeDtypeStruct((t // tm, 1, tm), jnp.int32)],
        scratch_shapes=[pltpu.VMEM((tm + 2 * POOL_HALO, pool_dim), jnp.float32)],
        compiler_params=_params(("parallel",)),
        name="mix",
    )(x, a, p, p, p, *weights)


def _rank_kernel(cls_ref, rank_ref, count_ref, carry_ref):
    tm = cls_ref.shape[2]

    @pl.when(pl.program_id(0) == 0)
    def _():
        carry_ref[...] = jnp.zeros_like(carry_ref)

    cls = cls_ref[0]
    onehot = lax.broadcasted_iota(jnp.int32, (CLASS_ROWS, tm), 0) == cls
    upper = (lax.broadcasted_iota(jnp.int32, (tm, tm), 0)
             <= lax.broadcasted_iota(jnp.int32, (tm, tm), 1)).astype(jnp.bfloat16)
    incl = jnp.dot(onehot.astype(jnp.bfloat16), upper, preferred_element_type=jnp.float32)
    before = incl - 1.0 + carry_ref[...]
    rank_ref[0] = jnp.sum(jnp.where(onehot, before, 0.0), axis=0, keepdims=True).astype(jnp.int32)
    carry_ref[...] = carry_ref[...] + incl[:, tm - 1:tm]
    count_ref[...] = carry_ref[...].astype(jnp.int32)


def _rank_call(cls):
    nt, _, tm = cls.shape
    return pl.pallas_call(
        _rank_kernel,
        grid=(nt,),
        in_specs=[pl.BlockSpec((1, 1, tm), lambda i: (i, 0, 0))],
        out_specs=[pl.BlockSpec((1, 1, tm), lambda i: (i, 0, 0)), _const_spec((CLASS_ROWS, 1))],
        out_shape=[jax.ShapeDtypeStruct((nt, 1, tm), jnp.int32),
                   jax.ShapeDtypeStruct((CLASS_ROWS, 1), jnp.int32)],
        scratch_shapes=[pltpu.VMEM((CLASS_ROWS, 1), jnp.float32)],
        compiler_params=_params(("arbitrary",)),
        name="rank",
    )(cls)


def _ffn_kernel(ea_ref, eb_ref, nvalid_ref, tok_ref, x_hbm, w1a_ref, b1a_ref, w2a_ref, b2a_ref,
                w1b_ref, b1b_ref, w2b_ref, b2b_ref, wr_ref, br_ref, lng_ref, lnb_ref,
                out_hbm, xbuf, obuf, sem_in, sem_out, *, alpha):
    blk = pl.program_id(0)
    n_valid = nvalid_ref[blk]
    n_rows = xbuf.shape[0]

    def row_in(r):
        return pltpu.make_async_copy(x_hbm.at[pl.ds(tok_ref[0, 0, r], 1)], xbuf.at[pl.ds(r, 1)], sem_in)

    def row_out(r):
        return pltpu.make_async_copy(obuf.at[pl.ds(r, 1)], out_hbm.at[pl.ds(tok_ref[0, 0, r], 1)], sem_out)

    @pl.when(n_valid > 0)
    def _():
        lax.fori_loop(0, n_rows, lambda r, c: (row_in(r).start(), c)[1], 0)
        lax.fori_loop(0, n_rows, lambda r, c: (row_in(r).wait(), c)[1], 0)

        x = xbuf[...]
        xb = x.astype(jnp.bfloat16)

        def expert(w1_ref, b1_ref, w2_ref, b2_ref):
            h = jnp.dot(xb, w1_ref[0], preferred_element_type=jnp.float32) + b1_ref[0]
            h = 0.5 * h * (1.0 + lax.erf(h * (2.0 ** -0.5)))
            return jnp.dot(h.astype(jnp.bfloat16), w2_ref[0], preferred_element_type=jnp.float32) + b2_ref[0]

        ea = ea_ref[blk]
        eb = eb_ref[blk]
        w_diff = wr_ref[pl.ds(eb, 1), :] - wr_ref[pl.ds(ea, 1), :]
        delta = jnp.sum(x * w_diff, axis=-1, keepdims=True) + (br_ref[eb] - br_ref[ea])
        gate_a = 1.0 / (1.0 + jnp.exp(delta))
        gate_b = 1.0 / (1.0 + jnp.exp(-delta))
        y = gate_a * expert(w1a_ref, b1a_ref, w2a_ref, b2a_ref) \
            + gate_b * expert(w1b_ref, b1b_ref, w2b_ref, b2b_ref)
        obuf[...] = _layer_norm(alpha * x + y, lng_ref[...], lnb_ref[...])

        lax.fori_loop(0, n_valid, lambda r, c: (row_out(r).start(), c)[1], 0)
        lax.fori_loop(0, n_valid, lambda r, c: (row_out(r).wait(), c)[1], 0)


def _ffn_call(x1, buf_tok, block_ea, block_eb, block_nvalid, lw, alpha):
    t, d = x1.shape
    n_blocks = block_ea.shape[0]
    d_ff = lw["w1"].shape[2]
    const = lambda shape: pl.BlockSpec(shape, lambda b, ea, eb, nv: (0,) * len(shape))
    by_a = lambda shape: pl.BlockSpec(shape, lambda b, ea, eb, nv: (ea[b], 0, 0))
    by_b = lambda shape: pl.BlockSpec(shape, lambda b, ea, eb, nv: (eb[b], 0, 0))
    grid_spec = pltpu.PrefetchScalarGridSpec(
        num_scalar_prefetch=3,
        grid=(n_blocks,),
        in_specs=[
            pl.BlockSpec((1, 1, MOE_BLOCK), lambda b, ea, eb, nv: (b, 0, 0), memory_space=pltpu.SMEM),
            pl.BlockSpec(memory_space=pl.ANY),
            by_a((1, d, d_ff)), by_a((1, 1, d_ff)), by_a((1, d_ff, d)), by_a((1, 1, d)),
            by_b((1, d, d_ff)), by_b((1, 1, d_ff)), by_b((1, d_ff, d)), by_b((1, 1, d)),
            const((N_EXPERTS, d)),
            pl.BlockSpec(memory_space=pltpu.SMEM),
            const((1, d)), const((1, d)),
        ],
        out_specs=pl.BlockSpec(memory_space=pl.ANY),
        scratch_shapes=[pltpu.VMEM((MOE_BLOCK, d), jnp.float32), pltpu.VMEM((MOE_BLOCK, d), jnp.float32),
                        pltpu.SemaphoreType.DMA(()), pltpu.SemaphoreType.DMA(())],
    )
    return pl.pallas_call(
        functools.partial(_ffn_kernel, alpha=alpha),
        grid_spec=grid_spec,
        out_shape=jax.ShapeDtypeStruct((t, d), jnp.float32),
        compiler_params=_params(("arbitrary",)),
        name="ffn",
    )(block_ea, block_eb, block_nvalid, buf_tok, x1,
      lw["w1"], lw["b1"], lw["w2"], lw["b2"], lw["w1"], lw["b1"], lw["w2"], lw["b2"],
      lw["w_router_rows"], lw["b_router"], lw["ln2_g"], lw["ln2_b"])


def _dispatch_tables(cls, rank, counts, t):
    counts = counts[:N_CLASSES, 0]
    n_blocks = -(-(t + N_CLASSES * (MOE_BLOCK - 1)) // MOE_BLOCK)
    padded = (counts + MOE_BLOCK - 1) // MOE_BLOCK * MOE_BLOCK
    pad_ends = jnp.cumsum(padded)
    pad_starts = pad_ends - padded
    dest = pad_starts[cls.reshape(t)] + rank.reshape(t)
    buf_tok = jnp.zeros((n_blocks * MOE_BLOCK,), jnp.int32).at[dest].set(jnp.arange(t, dtype=jnp.int32))
    block_start = jnp.arange(n_blocks, dtype=jnp.int32) * MOE_BLOCK
    block_cls = jnp.clip(jnp.searchsorted(pad_ends, block_start, side="right"), 0, N_CLASSES - 1)
    block_cls = block_cls.astype(jnp.int32)
    n_valid = jnp.clip(counts[block_cls] - (block_start - pad_starts[block_cls]), 0, MOE_BLOCK)
    pair = jnp.asarray(PAIRS, jnp.int32)[block_cls % len(PAIRS)]
    group = block_cls // len(PAIRS)
    ea = group * EXPERTS_PER_GROUP + pair[:, 0]
    eb = group * EXPERTS_PER_GROUP + pair[:, 1]
    return (buf_tok.reshape(n_blocks, 1, MOE_BLOCK), ea.astype(jnp.int32), eb.astype(jnp.int32),
            n_valid.astype(jnp.int32))


def kernel(x_prompt, x_sample, ln_in_g, ln_in_b, w_in, b_in, rpb, w_pool, b_pool, pool_scale, w_oa, w_op,
           w_out, b_out, ln1_g, ln1_b, w_router, b_router, w1, b1, w2, b2, ln2_g, ln2_b):
    depth, d, _ = w_in.shape
    attn_dim = N_HEADS * HEAD_DIM
    pool_dim = w_op.shape[1]
    qkvp = 3 * attn_dim + pool_dim
    alpha = (2 * depth) ** 0.25
    bf = jnp.bfloat16
    f32 = jnp.float32

    wr_hi = w_router.T.astype(bf)
    wr_lo = (w_router.T - wr_hi.astype(f32)).astype(bf)
    shared = {
        "w_router_t": jnp.concatenate([wr_hi, wr_lo], axis=0),
        "b_router_t": b_router.reshape(N_EXPERTS, 1).astype(f32),
        "w_router_rows": w_router.T.astype(f32),
        "b_router": b_router.astype(f32),
    }
    layers = []
    for l in range(depth):
        lw = dict(shared)
        lw.update({
            "w_qkvp": w_in[l, :, :qkvp].astype(bf), "b_qkvp": b_in[l, :qkvp].reshape(1, -1),
            "w_gate": w_in[l, :, qkvp:].astype(bf), "b_gate": b_in[l, qkvp:].reshape(1, -1),
            "table": _bias_table(rpb[l]),
            "w_pool": w_pool[l].astype(bf), "b_pool": b_pool[l].reshape(1, -1),
            "pool_scale": pool_scale[l].reshape(1, -1),
            "w_oa": w_oa[l].astype(bf), "w_op": w_op[l].astype(bf), "w_out": w_out[l].astype(bf),
            "b_out": b_out[l].reshape(1, -1),
            "ln1_g": ln1_g[l].reshape(1, -1), "ln1_b": ln1_b[l].reshape(1, -1),
            "w1": w1[l].astype(bf), "b1": b1[l].reshape(N_EXPERTS, 1, -1),
            "w2": w2[l].astype(bf), "b2": b2[l].reshape(N_EXPERTS, 1, -1),
            "ln2_g": ln2_g[l].reshape(1, -1), "ln2_b": ln2_b[l].reshape(1, -1),
        })
        layers.append(lw)

    def run(x3):
        batch, seq, _ = x3.shape
        t = batch * seq
        x = _ln_call(x3.reshape(t, d), ln_in_g.reshape(1, -1), ln_in_b.reshape(1, -1))
        for lw in layers:
            q, k, v, p = _proj_call(x, lw["w_qkvp"], lw["b_qkvp"], attn_dim, pool_dim)
            a = _attn_call(q, k, v, lw["table"], batch, seq)
            x1, cls = _mix_call(x, a, p, lw, seq, alpha)
            rank, counts = _rank_call(cls)
            buf_tok, ea, eb, n_valid = _dispatch_tables(cls, rank, counts, t)
            x = _ffn_call(x1, buf_tok, ea, eb, n_valid, lw, alpha)
        return x.reshape(batch, seq, d)

    return (run(x_prompt), run(x_sample))
```

```python
import functools

import jax
import jax.numpy as jnp
from jax import lax
from jax.experimental import pallas as pl
from jax.experimental.pallas import tpu as pltpu

GRID_W = 64
N_HEADS = 8
HEAD_DIM = 64
WIN_ROWS = 8
WIN_COLS = 16
POOL_WINDOWS = (2, 4, 8, 16)
N_EXPERTS = 16
N_GROUPS = 4
EXPERTS_PER_GROUP = 4
PAIRS = ((0, 1), (0, 2), (0, 3), (1, 2), (1, 3), (2, 3))
N_CLASSES = N_GROUPS * len(PAIRS)
CLASS_ROWS = 32
MOE_BLOCK = 256
LN_EPS = 1e-5
NEG = -1e30

LANES = 128
ROW_CHUNKS = 8
Q_ROWS = 8
KV_ROWS = 16
KV_BLK_ROWS = 4
TOK_TILE = 512
POOL_HALO = 8
VMEM_LIMIT = 56 * 1024 * 1024


def _layer_norm(x, g, b):
    mu = jnp.mean(x, axis=-1, keepdims=True)
    xc = x - mu
    var = jnp.mean(xc * xc, axis=-1, keepdims=True)
    return xc * lax.rsqrt(var + LN_EPS) * g + b


def _params(semantics):
    return pltpu.CompilerParams(dimension_semantics=semantics, vmem_limit_bytes=VMEM_LIMIT)


def _const_spec(shape):
    nd = len(shape)
    return pl.BlockSpec(shape, lambda *_: (0,) * nd)


def _load_rows(ref, n_tok):
    return jnp.concatenate([ref[pl.ds(s, n_tok, stride=ROW_CHUNKS), :] for s in range(ROW_CHUNKS)], axis=1)


def _store_rows(ref, value):
    n_tok = value.shape[0]
    for s in range(ROW_CHUNKS):
        ref[pl.ds(s, n_tok, stride=ROW_CHUNKS), :] = value[:, s * LANES:(s + 1) * LANES]


def _tok_spec(tm):
    return pl.BlockSpec((tm * ROW_CHUNKS, LANES), lambda i: (i, 0))


def _ln_kernel(x_ref, g_ref, b_ref, o_ref):
    _store_rows(o_ref, _layer_norm(x_ref[...], g_ref[...], b_ref[...]))


def _ln_call(x, g, b):
    t, d = x.shape
    assert d == ROW_CHUNKS * LANES
    tm = TOK_TILE
    return pl.pallas_call(
        _ln_kernel,
        grid=(t // tm,),
        in_specs=[pl.BlockSpec((tm, d), lambda i: (i, 0)), _const_spec((1, d)), _const_spec((1, d))],
        out_specs=_tok_spec(tm),
        out_shape=jax.ShapeDtypeStruct((t * ROW_CHUNKS, LANES), jnp.float32),
        compiler_params=_params(("parallel",)),
        name="ln_in",
    )(x, g, b)


def _proj_kernel(x_ref, w_ref, b_ref, q_ref, k_ref, v_ref, p_ref, *, attn_dim, scale):
    x = _load_rows(x_ref, q_ref.shape[0])
    acc = jnp.dot(x.astype(jnp.bfloat16), w_ref[...], preferred_element_type=jnp.float32)
    acc = acc + b_ref[...]
    q_ref[...] = (acc[:, :attn_dim] * scale).astype(jnp.bfloat16)
    k_ref[...] = acc[:, attn_dim:2 * attn_dim].astype(jnp.bfloat16)
    v_ref[...] = acc[:, 2 * attn_dim:3 * attn_dim].astype(jnp.bfloat16)
    p_ref[...] = acc[:, 3 * attn_dim:]


def _proj_call(x, t, w, b, attn_dim, pool_dim):
    d, n = w.shape
    tm = TOK_TILE
    row = lambda i: (i, 0)
    return pl.pallas_call(
        functools.partial(_proj_kernel, attn_dim=attn_dim, scale=HEAD_DIM ** -0.5),
        grid=(t // tm,),
        in_specs=[_tok_spec(tm), _const_spec((d, n)), _const_spec((1, n))],
        out_specs=[pl.BlockSpec((tm, attn_dim), row)] * 3 + [pl.BlockSpec((tm, pool_dim), row)],
        out_shape=[jax.ShapeDtypeStruct((t, attn_dim), jnp.bfloat16)] * 3
        + [jax.ShapeDtypeStruct((t, pool_dim), jnp.float32)],
        compiler_params=_params(("parallel",)),
        name="proj",
    )(x, w, b)


def _attn_kernel(q_ref, k0, k1, k2, k3, v0, v1, v2, v3, tab_ref, o_ref, *, rows):
    nq = Q_ROWS * GRID_W
    nk = KV_ROWS * GRID_W
    r0 = pl.program_id(1) * Q_ROWS
    q_row = lax.broadcasted_iota(jnp.int32, (nq, 1), 0) // GRID_W + r0
    k_row = lax.broadcasted_iota(jnp.int32, (1, nk), 1) // GRID_W + (r0 - (KV_ROWS - Q_ROWS) // 2)
    row_start = jnp.clip(q_row - WIN_ROWS // 2, 0, rows - WIN_ROWS)
    row_valid = (k_row >= row_start) & (k_row < row_start + WIN_ROWS)

    q = q_ref[...]
    k = jnp.concatenate([k0[...], k1[...], k2[...], k3[...]], axis=0)
    v = jnp.concatenate([v0[...], v1[...], v2[...], v3[...]], axis=0)
    for h in range(N_HEADS):
        lanes = slice(h * HEAD_DIM, (h + 1) * HEAD_DIM)
        s = lax.dot_general(q[:, lanes], k[:, lanes], (((1,), (1,)), ((), ())),
                            preferred_element_type=jnp.float32)
        s = jnp.where(row_valid, s + tab_ref[h], NEG)
        m = jnp.max(s, axis=-1, keepdims=True)
        e = jnp.exp(s - m)
        denom = jnp.sum(e, axis=-1, keepdims=True)
        o = jnp.dot(e.astype(jnp.bfloat16), v[:, lanes], preferred_element_type=jnp.float32)
        o_ref[:, lanes] = (o / denom).astype(o_ref.dtype)


def _attn_call(q, k, v, table, batch, seq):
    attn_dim = q.shape[1]
    rows = seq // GRID_W
    assert rows % Q_ROWS == 0 and rows >= WIN_ROWS
    nq = Q_ROWS * GRID_W
    kb = KV_BLK_ROWS * GRID_W
    q_blocks = seq // nq
    kv_blocks = seq // kb
    lead = (KV_ROWS - Q_ROWS) // 2 // KV_BLK_ROWS

    def kv_spec(j):
        def index(b, i):
            blk = jnp.clip(i * (Q_ROWS // KV_BLK_ROWS) - lead + j, 0, kv_blocks - 1)
            return (b * kv_blocks + blk, 0)
        return pl.BlockSpec((kb, attn_dim), index)

    kv_specs = [kv_spec(j) for j in range(KV_ROWS // KV_BLK_ROWS)]
    return pl.pallas_call(
        functools.partial(_attn_kernel, rows=rows),
        grid=(batch, q_blocks),
        in_specs=[pl.BlockSpec((nq, attn_dim), lambda b, i: (b * q_blocks + i, 0))]
        + kv_specs + kv_specs + [_const_spec(table.shape)],
        out_specs=pl.BlockSpec((nq, attn_dim), lambda b, i: (b * q_blocks + i, 0)),
        out_shape=jax.ShapeDtypeStruct(q.shape, jnp.bfloat16),
        compiler_params=_params(("parallel", "parallel")),
        name="attn",
    )(q, k, k, k, k, v, v, v, v, table)


def _bias_table(rpb):
    n_dr = 2 * WIN_ROWS - 1
    n_dc = 2 * WIN_COLS - 1
    c = jnp.arange(GRID_W)
    col_start = jnp.clip(c - WIN_COLS // 2, 0, GRID_W - WIN_COLS)
    col_ok = (c[None, :] >= col_start[:, None]) & (c[None, :] < col_start[:, None] + WIN_COLS)
    dc = c[None, :] - c[:, None] + (WIN_COLS - 1)
    onehot = ((dc[None] == jnp.arange(n_dc)[:, None, None]) & col_ok[None]).astype(jnp.float32)
    bank = jnp.einsum("hrd,dck->hrck", rpb.astype(jnp.float32), onehot, precision=lax.Precision.HIGHEST)
    bank = jnp.where(col_ok[None, None], bank, NEG)
    lead = (KV_ROWS - Q_ROWS) // 2
    before = max(0, lead + Q_ROWS - WIN_ROWS)
    after = max(0, KV_ROWS - lead + WIN_ROWS - 1 - n_dr)
    neg_tiles = lambda n: jnp.full((rpb.shape[0], n, GRID_W, GRID_W), NEG, jnp.float32)
    padded = jnp.concatenate([neg_tiles(before), bank, neg_tiles(after)], axis=1)
    rows_of_tiles = []
    for i in range(Q_ROWS):
        first = WIN_ROWS - 1 - lead - i + before
        tiles = padded[:, first:first + KV_ROWS]
        rows_of_tiles.append(jnp.transpose(tiles, (0, 2, 1, 3)))
    tab = jnp.stack(rows_of_tiles, axis=1)
    return tab.reshape(rpb.shape[0], Q_ROWS * GRID_W, KV_ROWS * GRID_W)


def _route_class(logits_t):
    m = jnp.max(logits_t, axis=0, keepdims=True)
    e = jnp.exp(logits_t - m)
    probs = e / jnp.sum(e, axis=0, keepdims=True)
    p = [probs[r:r + 1, :] for r in range(N_EXPERTS)]

    def top2_sum(a, b, c, d):
        hi1, lo1 = jnp.maximum(a, b), jnp.minimum(a, b)
        hi2, lo2 = jnp.maximum(c, d), jnp.minimum(c, d)
        return jnp.maximum(hi1, hi2) + jnp.maximum(jnp.minimum(hi1, hi2), jnp.maximum(lo1, lo2))

    score = [top2_sum(*p[4 * g:4 * g + 4]) for g in range(N_GROUPS)]
    best_g = jnp.zeros_like(score[0], dtype=jnp.int32)
    best_s = score[0]
    for g in range(1, N_GROUPS):
        upd = score[g] > best_s
        best_g = jnp.where(upd, g, best_g)
        best_s = jnp.where(upd, score[g], best_s)

    vals = []
    for r in range(EXPERTS_PER_GROUP):
        val = p[r]
        for g in range(1, N_GROUPS):
            val = jnp.where(best_g == g, p[4 * g + r], val)
        vals.append(val)

    def argmax4(vs):
        idx = jnp.zeros_like(best_g)
        top = vs[0]
        for r in range(1, EXPERTS_PER_GROUP):
            upd = vs[r] > top
            idx = jnp.where(upd, r, idx)
            top = jnp.where(upd, vs[r], top)
        return idx

    i1 = argmax4(vals)
    i2 = argmax4([jnp.where(i1 == r, -1.0, vals[r]) for r in range(EXPERTS_PER_GROUP)])
    lo = jnp.minimum(i1, i2)
    hi = jnp.maximum(i1, i2)
    pair = jnp.zeros_like(lo)
    for n, (a, b) in enumerate(PAIRS):
        pair = jnp.where((lo == a) & (hi == b), n, pair)
    return best_g * len(PAIRS) + pair


def _mix_kernel(x_ref, a_ref, p_ref, pprev_ref, pnext_ref, wg_ref, bg_ref, wpool_ref, bpool_ref,
                pscale_ref, woa_ref, wop_ref, wout_ref, bout_ref, lng_ref, lnb_ref, wr_ref, br_ref,
                x1_ref, cls_ref, halo_ref, *, seq, alpha):
    tm = a_ref.shape[0]
    d = wout_ref.shape[1]
    pool_dim = p_ref.shape[1]
    group_dim = pool_dim // len(POOL_WINDOWS)
    tiles_per_seq = seq // tm
    i = pl.program_id(0)
    tile_in_seq = i % tiles_per_seq
    at_start = tile_in_seq == 0
    at_end = tile_in_seq == tiles_per_seq - 1

    x = _load_rows(x_ref, tm)
    xb = x.astype(jnp.bfloat16)
    gates = jax.nn.sigmoid(jnp.dot(xb, wg_ref[...], preferred_element_type=jnp.float32) + bg_ref[...])

    halo_ref[0:POOL_HALO, :] = jnp.where(at_start, 0.0, pprev_ref[...])
    halo_ref[POOL_HALO:POOL_HALO + tm, :] = p_ref[...]
    halo_ref[POOL_HALO + tm:, :] = jnp.where(at_end, 0.0, pnext_ref[...])
    pos = tile_in_seq * tm + lax.broadcasted_iota(jnp.int32, (tm, 1), 0)
    pooled = []
    for g, w in enumerate(POOL_WINDOWS):
        lanes = slice(g * group_dim, (g + 1) * group_dim)
        total = halo_ref[POOL_HALO - w // 2:POOL_HALO - w // 2 + tm, lanes]
        for off in range(-w // 2 + 1, w // 2):
            total = total + halo_ref[POOL_HALO + off:POOL_HALO + off + tm, lanes]
        count = jnp.clip(pos + w // 2, 0, seq) - jnp.clip(pos - w // 2, 0, seq)
        diff = total / count.astype(jnp.float32) - p_ref[:, lanes]
        y = jnp.dot(diff.astype(jnp.bfloat16), wpool_ref[g], preferred_element_type=jnp.float32)
        pooled.append((y + bpool_ref[:, lanes]) * pscale_ref[:, lanes])
    pm = jnp.concatenate(pooled, axis=1)

    attn_out = jnp.dot(a_ref[...], woa_ref[...], preferred_element_type=jnp.float32)
    pool_out = jnp.dot(pm.astype(jnp.bfloat16), wop_ref[...], preferred_element_type=jnp.float32)
    mix = gates[:, :d] * attn_out + gates[:, d:] * pool_out
    y = jnp.dot(mix.astype(jnp.bfloat16), wout_ref[...], preferred_element_type=jnp.float32) + bout_ref[...]
    x1 = _layer_norm(alpha * x + y, lng_ref[...], lnb_ref[...])
    _store_rows(x1_ref, x1)

    x_hi = x1.astype(jnp.bfloat16)
    x_lo = (x1 - x_hi.astype(jnp.float32)).astype(jnp.bfloat16)
    nt = (((1,), (1,)), ((), ()))
    both = lax.dot_general(wr_ref[...], x_hi, nt, preferred_element_type=jnp.float32)
    cross = lax.dot_general(wr_ref[0:N_EXPERTS, :], x_lo, nt, preferred_element_type=jnp.float32)
    logits_t = both[0:N_EXPERTS, :] + both[N_EXPERTS:, :] + cross + br_ref[...]
    cls_ref[0] = _route_class(logits_t)


def _mix_call(x, a, p, lw, seq, alpha):
    t, attn_dim = a.shape
    tm = TOK_TILE
    assert seq % tm == 0 and tm % POOL_HALO == 0
    pool_dim = p.shape[1]
    row = lambda i: (i, 0)
    halo_blocks = t // POOL_HALO
    per_tile = tm // POOL_HALO
    prev = lambda i: (jnp.maximum(i * per_tile - 1, 0), 0)
    nxt = lambda i: (jnp.minimum((i + 1) * per_tile, halo_blocks - 1), 0)
    weights = [lw["w_gate"], lw["b_gate"], lw["w_pool"], lw["b_pool"], lw["pool_scale"], lw["w_oa"],
               lw["w_op"], lw["w_out"], lw["b_out"], lw["ln1_g"], lw["ln1_b"], lw["w_router_t"],
               lw["b_router_t"]]
    return pl.pallas_call(
        functools.partial(_mix_kernel, seq=seq, alpha=alpha),
        grid=(t // tm,),
        in_specs=[_tok_spec(tm), pl.BlockSpec((tm, attn_dim), row),
                  pl.BlockSpec((tm, pool_dim), row), pl.BlockSpec((POOL_HALO, pool_dim), prev),
                  pl.BlockSpec((POOL_HALO, pool_dim), nxt)] + [_const_spec(w.shape) for w in weights],
        out_specs=[_tok_spec(tm), pl.BlockSpec((1, 1, tm), lambda i: (i, 0, 0))],
        out_shape=[jax.ShapeDtypeStruct((t * ROW_CHUNKS, LANES), jnp.float32),
                   jax.ShapeDtypeStruct((t // tm, 1, tm), jnp.int32)],
        scratch_shapes=[pltpu.VMEM((tm + 2 * POOL_HALO, pool_dim), jnp.float32)],
        compiler_params=_params(("parallel",)),
        name="mix",
    )(x, a, p, p, p, *weights)


def _rank_kernel(cls_ref, rank_ref, count_ref, carry_ref):
    tm = cls_ref.shape[2]

    @pl.when(pl.program_id(0) == 0)
    def _():
        carry_ref[...] = jnp.zeros_like(carry_ref)

    cls = cls_ref[0]
    onehot = lax.broadcasted_iota(jnp.int32, (CLASS_ROWS, tm), 0) == cls
    upper = (lax.broadcasted_iota(jnp.int32, (tm, tm), 0)
             <= lax.broadcasted_iota(jnp.int32, (tm, tm), 1)).astype(jnp.bfloat16)
    incl = jnp.dot(onehot.astype(jnp.bfloat16), upper, preferred_element_type=jnp.float32)
    before = incl - 1.0 + carry_ref[...]
    rank_ref[0] = jnp.sum(jnp.where(onehot, before, 0.0), axis=0, keepdims=True).astype(jnp.int32)
    carry_ref[...] = carry_ref[...] + incl[:, tm - 1:tm]
    count_ref[...] = carry_ref[...].astype(jnp.int32)


def _rank_call(cls):
    nt, _, tm = cls.shape
    return pl.pallas_call(
        _rank_kernel,
        grid=(nt,),
        in_specs=[pl.BlockSpec((1, 1, tm), lambda i: (i, 0, 0))],
        out_specs=[pl.BlockSpec((1, 1, tm), lambda i: (i, 0, 0)), _const_spec((CLASS_ROWS, 1))],
        out_shape=[jax.ShapeDtypeStruct((nt, 1, tm), jnp.int32),
                   jax.ShapeDtypeStruct((CLASS_ROWS, 1), jnp.int32)],
        scratch_shapes=[pltpu.VMEM((CLASS_ROWS, 1), jnp.float32)],
        compiler_params=_params(("arbitrary",)),
        name="rank",
    )(cls)


def _ffn_kernel(ea_ref, eb_ref, nvalid_ref, tok_ref, tok_next_ref, tok_prev_ref, x_hbm,
                w1a_ref, b1a_ref, w2a_ref, b2a_ref, w1b_ref, b1b_ref, w2b_ref, b2b_ref,
                wr_ref, br_ref, lng_ref, lnb_ref, out_hbm, xbuf, obuf, sem_in, sem_out,
                *, alpha, n_tok):
    blk = pl.program_id(0)
    last = pl.num_programs(0) - 1
    slot = blk % 2
    other = 1 - slot
    n_valid = nvalid_ref[blk]
    n_valid_prev = jnp.where(blk > 0, nvalid_ref[jnp.maximum(blk - 1, 0)], 0)
    rows = MOE_BLOCK
    tile = ROW_CHUNKS

    def gather_row(ids_ref, dst_slot, r):
        src = pl.multiple_of(ids_ref[0, 0, r] * tile, tile)
        return pltpu.make_async_copy(x_hbm.at[pl.ds(src, tile)], xbuf.at[dst_slot, pl.ds(r * tile, tile)],
                                     sem_in.at[dst_slot])

    def scatter_row(ids_ref, count, src_slot, r):
        token = jnp.where(r < count, ids_ref[0, 0, r], n_tok + r)
        dst = pl.multiple_of(token * tile, tile)
        return pltpu.make_async_copy(obuf.at[src_slot, pl.ds(r * tile, tile)], out_hbm.at[pl.ds(dst, tile)],
                                     sem_out.at[src_slot])

    def wait_gather(dst_slot):
        pltpu.make_async_copy(x_hbm.at[pl.ds(0, rows * tile)], xbuf.at[dst_slot], sem_in.at[dst_slot]).wait()

    def wait_scatter(src_slot):
        pltpu.make_async_copy(obuf.at[src_slot], out_hbm.at[pl.ds(0, rows * tile)], sem_out.at[src_slot]).wait()

    def start_all(make_copy):
        lax.fori_loop(0, rows, lambda r, c: (make_copy(r).start(), c)[1], 0, unroll=8)

    @pl.when(blk == 0)
    def _():
        obuf[1] = jnp.zeros(obuf.shape[1:], obuf.dtype)
        start_all(functools.partial(gather_row, tok_ref, 0))

    wait_gather(slot)
    start_all(functools.partial(gather_row, tok_next_ref, other))
    start_all(functools.partial(scatter_row, tok_prev_ref, n_valid_prev, other))

    @pl.when(n_valid > 0)
    def _():
        x = _load_rows(xbuf.at[slot], rows)
        xb = x.astype(jnp.bfloat16)

        def expert(w1_ref, b1_ref, w2_ref, b2_ref):
            h = jnp.dot(xb, w1_ref[0], preferred_element_type=jnp.float32) + b1_ref[0]
            h = 0.5 * h * (1.0 + lax.erf(h * (2.0 ** -0.5)))
            return jnp.dot(h.astype(jnp.bfloat16), w2_ref[0], preferred_element_type=jnp.float32) + b2_ref[0]

        ea = ea_ref[blk]
        eb = eb_ref[blk]
        w_diff = wr_ref[pl.ds(eb, 1), :] - wr_ref[pl.ds(ea, 1), :]
        delta = jnp.sum(x * w_diff, axis=-1, keepdims=True) + (br_ref[eb] - br_ref[ea])
        gate_a = 1.0 / (1.0 + jnp.exp(delta))
        gate_b = 1.0 / (1.0 + jnp.exp(-delta))
        y = gate_a * expert(w1a_ref, b1a_ref, w2a_ref, b2a_ref) \
            + gate_b * expert(w1b_ref, b1b_ref, w2b_ref, b2b_ref)
        _store_rows(obuf.at[slot], _layer_norm(alpha * x + y, lng_ref[...], lnb_ref[...]))

    wait_scatter(other)

    @pl.when(blk == last)
    def _():
        wait_gather(other)
        start_all(functools.partial(scatter_row, tok_ref, n_valid, slot))
        wait_scatter(slot)


def _ffn_call(x1, t, buf_tok, block_ea, block_eb, block_nvalid, lw, alpha):
    n_blocks = block_ea.shape[0]
    _, d, d_ff = lw["w1"].shape
    const = lambda shape: pl.BlockSpec(shape, lambda b, ea, eb, nv: (0,) * len(shape))
    by_a = lambda shape: pl.BlockSpec(shape, lambda b, ea, eb, nv: (ea[b], 0, 0))
    by_b = lambda shape: pl.BlockSpec(shape, lambda b, ea, eb, nv: (eb[b], 0, 0))
    ids = lambda shift: pl.BlockSpec(
        (1, 1, MOE_BLOCK), lambda b, ea, eb, nv: (jnp.clip(b + shift, 0, n_blocks - 1), 0, 0),
        memory_space=pltpu.SMEM)
    grid_spec = pltpu.PrefetchScalarGridSpec(
        num_scalar_prefetch=3,
        grid=(n_blocks,),
        in_specs=[
            ids(0), ids(1), ids(-1),
            pl.BlockSpec(memory_space=pl.ANY),
            by_a((1, d, d_ff)), by_a((1, 1, d_ff)), by_a((1, d_ff, d)), by_a((1, 1, d)),
            by_b((1, d, d_ff)), by_b((1, 1, d_ff)), by_b((1, d_ff, d)), by_b((1, 1, d)),
            const((N_EXPERTS, d)),
            pl.BlockSpec(memory_space=pltpu.SMEM),
            const((1, d)), const((1, d)),
        ],
        out_specs=pl.BlockSpec(memory_space=pl.ANY),
        scratch_shapes=[pltpu.VMEM((2, MOE_BLOCK * ROW_CHUNKS, LANES), jnp.float32),
                        pltpu.VMEM((2, MOE_BLOCK * ROW_CHUNKS, LANES), jnp.float32),
                        pltpu.SemaphoreType.DMA((2,)), pltpu.SemaphoreType.DMA((2,))],
    )
    return pl.pallas_call(
        functools.partial(_ffn_kernel, alpha=alpha, n_tok=t),
        grid_spec=grid_spec,
        out_shape=jax.ShapeDtypeStruct(((t + MOE_BLOCK) * ROW_CHUNKS, LANES), jnp.float32),
        compiler_params=_params(("arbitrary",)),
        name="ffn",
    )(block_ea, block_eb, block_nvalid, buf_tok, buf_tok, buf_tok, x1,
      lw["w1"], lw["b1"], lw["w2"], lw["b2"], lw["w1"], lw["b1"], lw["w2"], lw["b2"],
      lw["w_router_rows"], lw["b_router"], lw["ln2_g"], lw["ln2_b"])


def _dispatch_tables(cls, rank, counts, t):
    counts = counts[:N_CLASSES, 0]
    n_blocks = -(-(t + N_CLASSES * (MOE_BLOCK - 1)) // MOE_BLOCK)
    padded = (counts + MOE_BLOCK - 1) // MOE_BLOCK * MOE_BLOCK
    pad_ends = jnp.cumsum(padded)
    pad_starts = pad_ends - padded
    dest = pad_starts[cls.reshape(t)] + rank.reshape(t)
    buf_tok = jnp.zeros((n_blocks * MOE_BLOCK,), jnp.int32).at[dest].set(
        jnp.arange(t, dtype=jnp.int32), unique_indices=True, indices_are_sorted=False)
    block_start = jnp.arange(n_blocks, dtype=jnp.int32) * MOE_BLOCK
    block_cls = jnp.minimum(jnp.sum(pad_ends[None, :] <= block_start[:, None], axis=1), N_CLASSES - 1)
    block_cls = block_cls.astype(jnp.int32)
    n_valid = jnp.clip(counts[block_cls] - (block_start - pad_starts[block_cls]), 0, MOE_BLOCK)
    pair = jnp.asarray(PAIRS, jnp.int32)[block_cls % len(PAIRS)]
    group = block_cls // len(PAIRS)
    ea = group * EXPERTS_PER_GROUP + pair[:, 0]
    eb = group * EXPERTS_PER_GROUP + pair[:, 1]
    return (buf_tok.reshape(n_blocks, 1, MOE_BLOCK), ea.astype(jnp.int32), eb.astype(jnp.int32),
            n_valid.astype(jnp.int32))


def kernel(x_prompt, x_sample, ln_in_g, ln_in_b, w_in, b_in, rpb, w_pool, b_pool, pool_scale, w_oa, w_op,
           w_out, b_out, ln1_g, ln1_b, w_router, b_router, w1, b1, w2, b2, ln2_g, ln2_b):
    depth, d, _ = w_in.shape
    attn_dim = N_HEADS * HEAD_DIM
    pool_dim = w_op.shape[1]
    qkvp = 3 * attn_dim + pool_dim
    alpha = (2 * depth) ** 0.25
    bf = jnp.bfloat16
    f32 = jnp.float32

    wr_hi = w_router.T.astype(bf)
    wr_lo = (w_router.T - wr_hi.astype(f32)).astype(bf)
    shared = {
        "w_router_t": jnp.concatenate([wr_hi, wr_lo], axis=0),
        "b_router_t": b_router.reshape(N_EXPERTS, 1).astype(f32),
        "w_router_rows": w_router.T.astype(f32),
        "b_router": b_router.astype(f32),
    }
    layers = []
    for l in range(depth):
        lw = dict(shared)
        lw.update({
            "w_qkvp": w_in[l, :, :qkvp].astype(bf), "b_qkvp": b_in[l, :qkvp].reshape(1, -1),
            "w_gate": w_in[l, :, qkvp:].astype(bf), "b_gate": b_in[l, qkvp:].reshape(1, -1),
            "table": _bias_table(rpb[l]),
            "w_pool": w_pool[l].astype(bf), "b_pool": b_pool[l].reshape(1, -1),
            "pool_scale": pool_scale[l].reshape(1, -1),
            "w_oa": w_oa[l].astype(bf), "w_op": w_op[l].astype(bf), "w_out": w_out[l].astype(bf),
            "b_out": b_out[l].reshape(1, -1),
            "ln1_g": ln1_g[l].reshape(1, -1), "ln1_b": ln1_b[l].reshape(1, -1),
            "w1": w1[l].astype(bf), "b1": b1[l].reshape(N_EXPERTS, 1, -1),
            "w2": w2[l].astype(bf), "b2": b2[l].reshape(N_EXPERTS, 1, -1),
            "ln2_g": ln2_g[l].reshape(1, -1), "ln2_b": ln2_b[l].reshape(1, -1),
        })
        layers.append(lw)

    def run(x3):
        batch, seq, _ = x3.shape
        t = batch * seq
        x = _ln_call(x3.reshape(t, d), ln_in_g.reshape(1, -1), ln_in_b.reshape(1, -1))
        for lw in layers:
            q, k, v, p = _proj_call(x, t, lw["w_qkvp"], lw["b_qkvp"], attn_dim, pool_dim)
            a = _attn_call(q, k, v, lw["table"], batch, seq)
            x1, cls = _mix_call(x, a, p, lw, seq, alpha)
            rank, counts = _rank_call(cls)
            buf_tok, ea, eb, n_valid = _dispatch_tables(cls, rank, counts, t)
            x = _ffn_call(x1, t, buf_tok, ea, eb, n_valid, lw, alpha)
        return x[:t * ROW_CHUNKS].reshape(batch, seq, d)

    return (run(x_prompt), run(x_sample))
```

```python
import functools

import jax
import jax.numpy as jnp
from jax import lax
from jax.experimental import pallas as pl
from jax.experimental.pallas import tpu as pltpu

GRID_W = 64
N_HEADS = 8
HEAD_DIM = 64
WIN_ROWS = 8
WIN_COLS = 16
POOL_WINDOWS = (2, 4, 8, 16)
N_EXPERTS = 16
N_GROUPS = 4
EXPERTS_PER_GROUP = 4
PAIRS = ((0, 1), (0, 2), (0, 3), (1, 2), (1, 3), (2, 3))
N_CLASSES = N_GROUPS * len(PAIRS)
CLASS_ROWS = 32
MOE_BLOCK = 256
LN_EPS = 1e-5
NEG = -1e30

LANES = 128
ROW_CHUNKS = 8
LOG2_E = 1.4426950408889634
Q_ROWS = 4
KV_ROWS = 12
KV_BLK_ROWS = 4
TOK_TILE = 512
POOL_HALO = 8
VMEM_LIMIT = 56 * 1024 * 1024


def _layer_norm(x, g, b):
    mu = jnp.mean(x, axis=-1, keepdims=True)
    xc = x - mu
    var = jnp.mean(xc * xc, axis=-1, keepdims=True)
    return xc * lax.rsqrt(var + LN_EPS) * g + b


def _params(semantics):
    return pltpu.CompilerParams(dimension_semantics=semantics, vmem_limit_bytes=VMEM_LIMIT)


def _const_spec(shape):
    nd = len(shape)
    return pl.BlockSpec(shape, lambda *_: (0,) * nd)


def _load_rows(ref, n_tok):
    return jnp.concatenate([ref[pl.ds(s, n_tok, stride=ROW_CHUNKS), :] for s in range(ROW_CHUNKS)], axis=1)


def _store_rows(ref, value):
    n_tok = value.shape[0]
    for s in range(ROW_CHUNKS):
        ref[pl.ds(s, n_tok, stride=ROW_CHUNKS), :] = value[:, s * LANES:(s + 1) * LANES]


def _tok_spec(tm):
    return pl.BlockSpec((tm * ROW_CHUNKS, LANES), lambda i: (i, 0))


def _ln_kernel(x_ref, g_ref, b_ref, o_ref):
    _store_rows(o_ref, _layer_norm(x_ref[...], g_ref[...], b_ref[...]))


def _ln_call(x, g, b):
    t, d = x.shape
    assert d == ROW_CHUNKS * LANES
    tm = TOK_TILE
    return pl.pallas_call(
        _ln_kernel,
        grid=(t // tm,),
        in_specs=[pl.BlockSpec((tm, d), lambda i: (i, 0)), _const_spec((1, d)), _const_spec((1, d))],
        out_specs=_tok_spec(tm),
        out_shape=jax.ShapeDtypeStruct((t * ROW_CHUNKS, LANES), jnp.float32),
        compiler_params=_params(("parallel",)),
        name="ln_in",
    )(x, g, b)


def _unpack_kernel(x_ref, o_ref):
    o_ref[...] = _load_rows(x_ref, o_ref.shape[0])


def _unpack_call(x, t):
    tm = TOK_TILE
    d = ROW_CHUNKS * LANES
    return pl.pallas_call(
        _unpack_kernel,
        grid=(t // tm,),
        in_specs=[_tok_spec(tm)],
        out_specs=pl.BlockSpec((tm, d), lambda i: (i, 0)),
        out_shape=jax.ShapeDtypeStruct((t, d), jnp.float32),
        compiler_params=_params(("parallel",)),
        name="unpack",
    )(x)


def _proj_kernel(x_ref, w_ref, b_ref, q_ref, k_ref, v_ref, p_ref, *, attn_dim, scale):
    x = _load_rows(x_ref, q_ref.shape[0])
    acc = jnp.dot(x.astype(jnp.bfloat16), w_ref[...], preferred_element_type=jnp.float32)
    acc = acc + b_ref[...]
    q_ref[...] = (acc[:, :attn_dim] * scale).astype(jnp.bfloat16)
    k_ref[...] = acc[:, attn_dim:2 * attn_dim].astype(jnp.bfloat16)
    v_ref[...] = acc[:, 2 * attn_dim:3 * attn_dim].astype(jnp.bfloat16)
    p_ref[...] = acc[:, 3 * attn_dim:]


def _proj_call(x, t, w, b, attn_dim, pool_dim):
    d, n = w.shape
    tm = TOK_TILE
    row = lambda i: (i, 0)
    return pl.pallas_call(
        functools.partial(_proj_kernel, attn_dim=attn_dim, scale=HEAD_DIM ** -0.5 * LOG2_E),
        grid=(t // tm,),
        in_specs=[_tok_spec(tm), _const_spec((d, n)), _const_spec((1, n))],
        out_specs=[pl.BlockSpec((tm, attn_dim), row)] * 3 + [pl.BlockSpec((tm, pool_dim), row)],
        out_shape=[jax.ShapeDtypeStruct((t, attn_dim), jnp.bfloat16)] * 3
        + [jax.ShapeDtypeStruct((t, pool_dim), jnp.float32)],
        compiler_params=_params(("parallel",)),
        name="proj",
    )(x, w, b)


def _attn_kernel(q_ref, k0, k1, k2, v0, v1, v2, tab_ref, o_ref):
    q = q_ref[...]
    k = jnp.concatenate([k0[...], k1[...], k2[...]], axis=0)
    v = jnp.concatenate([v0[...], v1[...], v2[...]], axis=0)
    for h in range(N_HEADS):
        lanes = slice(h * HEAD_DIM, (h + 1) * HEAD_DIM)
        s = lax.dot_general(q[:, lanes], k[:, lanes], (((1,), (1,)), ((), ())),
                            preferred_element_type=jnp.float32)
        s = s + tab_ref[0, h]
        m = jnp.max(s, axis=-1, keepdims=True)
        e = jnp.exp2(s - m)
        denom = jnp.sum(e, axis=-1, keepdims=True)
        o = jnp.dot(e.astype(jnp.bfloat16), v[:, lanes], preferred_element_type=jnp.float32)
        o_ref[:, lanes] = (o / denom).astype(o_ref.dtype)


def _attn_call(q, k, v, table, batch, seq):
    attn_dim = q.shape[1]
    rows = seq // GRID_W
    assert rows % Q_ROWS == 0 and rows >= WIN_ROWS
    assert Q_ROWS == KV_BLK_ROWS == WIN_ROWS // 2 and KV_ROWS == 3 * KV_BLK_ROWS
    nq = Q_ROWS * GRID_W
    q_blocks = seq // nq

    def kv_spec(j):
        def index(i, b):
            return (b * q_blocks + jnp.clip(i - 1 + j, 0, q_blocks - 1), 0)
        return pl.BlockSpec((nq, attn_dim), index)

    def table_index(i, b):
        return (jnp.where(i == 0, 0, jnp.where(i == q_blocks - 1, 2, 1)), 0, 0, 0)

    kv_specs = [kv_spec(j) for j in range(KV_ROWS // KV_BLK_ROWS)]
    q_spec = pl.BlockSpec((nq, attn_dim), lambda i, b: (b * q_blocks + i, 0))
    return pl.pallas_call(
        _attn_kernel,
        grid=(q_blocks, batch),
        in_specs=[q_spec] + kv_specs + kv_specs + [pl.BlockSpec((1,) + table.shape[1:], table_index)],
        out_specs=q_spec,
        out_shape=jax.ShapeDtypeStruct(q.shape, jnp.bfloat16),
        compiler_params=_params(("parallel", "parallel")),
        name="attn",
    )(q, k, k, k, v, v, v, table)


def _bias_table(rpb):
    n_dr = 2 * WIN_ROWS - 1
    n_dc = 2 * WIN_COLS - 1
    n_heads = rpb.shape[0]
    c = jnp.arange(GRID_W)
    col_start = jnp.clip(c - WIN_COLS // 2, 0, GRID_W - WIN_COLS)
    col_ok = (c[None, :] >= col_start[:, None]) & (c[None, :] < col_start[:, None] + WIN_COLS)
    dc = c[None, :] - c[:, None] + (WIN_COLS - 1)
    onehot = ((dc[None] == jnp.arange(n_dc)[:, None, None]) & col_ok[None]).astype(jnp.float32)
    bank = jnp.einsum("hrd,dck->hrck", rpb.astype(jnp.float32) * LOG2_E, onehot,
                      precision=lax.Precision.HIGHEST)
    lead = (KV_ROWS - Q_ROWS) // 2
    i = jnp.arange(Q_ROWS)[:, None]
    j = jnp.arange(KV_ROWS)[None, :]
    row_ok = jnp.stack([(j >= lead) & (j < lead + WIN_ROWS) & (i >= 0),
                        (j - i >= 0) & (j - i < WIN_ROWS),
                        (j < WIN_ROWS) & (i >= 0)])
    first = [WIN_ROWS - 1 - lead - qi for qi in range(Q_ROWS)]
    assert min(first) >= 0 and max(first) + KV_ROWS <= n_dr
    tiles = jnp.stack([bank[:, f:f + KV_ROWS] for f in first], axis=1)
    ok = row_ok[:, None, :, :, None, None] & col_ok[None, None, None, None]
    tab = jnp.where(ok, tiles[None], NEG)
    tab = jnp.transpose(tab, (0, 1, 2, 4, 3, 5))
    return tab.reshape(3, n_heads, Q_ROWS * GRID_W, KV_ROWS * GRID_W)


def _route_class(logits_t):
    m = jnp.max(logits_t, axis=0, keepdims=True)
    e = jnp.exp(logits_t - m)
    probs = e / jnp.sum(e, axis=0, keepdims=True)
    p = [probs[r:r + 1, :] for r in range(N_EXPERTS)]

    def top2_sum(a, b, c, d):
        hi1, lo1 = jnp.maximum(a, b), jnp.minimum(a, b)
        hi2, lo2 = jnp.maximum(c, d), jnp.minimum(c, d)
        return jnp.maximum(hi1, hi2) + jnp.maximum(jnp.minimum(hi1, hi2), jnp.maximum(lo1, lo2))

    score = [top2_sum(*p[4 * g:4 * g + 4]) for g in range(N_GROUPS)]
    best_g = jnp.zeros_like(score[0], dtype=jnp.int32)
    best_s = score[0]
    for g in range(1, N_GROUPS):
        upd = score[g] > best_s
        best_g = jnp.where(upd, g, best_g)
        best_s = jnp.where(upd, score[g], best_s)

    vals = []
    for r in range(EXPERTS_PER_GROUP):
        val = p[r]
        for g in range(1, N_GROUPS):
            val = jnp.where(best_g == g, p[4 * g + r], val)
        vals.append(val)

    def argmax4(vs):
        idx = jnp.zeros_like(best_g)
        top = vs[0]
        for r in range(1, EXPERTS_PER_GROUP):
            upd = vs[r] > top
            idx = jnp.where(upd, r, idx)
            top = jnp.where(upd, vs[r], top)
        return idx

    i1 = argmax4(vals)
    i2 = argmax4([jnp.where(i1 == r, -1.0, vals[r]) for r in range(EXPERTS_PER_GROUP)])
    lo = jnp.minimum(i1, i2)
    hi = jnp.maximum(i1, i2)
    pair = jnp.zeros_like(lo)
    for n, (a, b) in enumerate(PAIRS):
        pair = jnp.where((lo == a) & (hi == b), n, pair)
    return best_g * len(PAIRS) + pair


def _mix_kernel(x_ref, a_ref, p_ref, pprev_ref, pnext_ref, wg_ref, bg_ref, wpool_ref, bpool_ref,
                pscale_ref, woa_ref, wop_ref, wout_ref, bout_ref, lng_ref, lnb_ref, wr_ref, br_ref,
                x1_ref, cls_ref, halo_ref, *, seq, alpha):
    tm = a_ref.shape[0]
    d = wout_ref.shape[1]
    pool_dim = p_ref.shape[1]
    group_dim = pool_dim // len(POOL_WINDOWS)
    tiles_per_seq = seq // tm
    i = pl.program_id(0)
    tile_in_seq = i % tiles_per_seq
    at_start = tile_in_seq == 0
    at_end = tile_in_seq == tiles_per_seq - 1

    x = _load_rows(x_ref, tm)
    xb = x.astype(jnp.bfloat16)
    z = jnp.dot(xb, wg_ref[...], preferred_element_type=jnp.float32) + bg_ref[...]
    gates = 0.5 * jnp.tanh(0.5 * z) + 0.5

    halo_ref[0:POOL_HALO, :] = jnp.where(at_start, 0.0, pprev_ref[...])
    halo_ref[POOL_HALO:POOL_HALO + tm, :] = p_ref[...]
    halo_ref[POOL_HALO + tm:, :] = jnp.where(at_end, 0.0, pnext_ref[...])
    pos = tile_in_seq * tm + lax.broadcasted_iota(jnp.int32, (tm, 1), 0)
    pooled = []
    for g, w in enumerate(POOL_WINDOWS):
        lanes = slice(g * group_dim, (g + 1) * group_dim)
        total = halo_ref[POOL_HALO - w // 2:POOL_HALO - w // 2 + tm, lanes]
        for off in range(-w // 2 + 1, w // 2):
            total = total + halo_ref[POOL_HALO + off:POOL_HALO + off + tm, lanes]
        count = jnp.clip(pos + w // 2, 0, seq) - jnp.clip(pos - w // 2, 0, seq)
        diff = total / count.astype(jnp.float32) - p_ref[:, lanes]
        y = jnp.dot(diff.astype(jnp.bfloat16), wpool_ref[g], preferred_element_type=jnp.float32)
        pooled.append((y + bpool_ref[:, lanes]) * pscale_ref[:, lanes])
    pm = jnp.concatenate(pooled, axis=1)

    attn_out = jnp.dot(a_ref[...], woa_ref[...], preferred_element_type=jnp.float32)
    pool_out = jnp.dot(pm.astype(jnp.bfloat16), wop_ref[...], preferred_element_type=jnp.float32)
    mix = gates[:, :d] * attn_out + gates[:, d:] * pool_out
    y = jnp.dot(mix.astype(jnp.bfloat16), wout_ref[...], preferred_element_type=jnp.float32) + bout_ref[...]
    x1 = _layer_norm(alpha * x + y, lng_ref[...], lnb_ref[...])
    _store_rows(x1_ref, x1)

    x_hi = x1.astype(jnp.bfloat16)
    x_lo = (x1 - x_hi.astype(jnp.float32)).astype(jnp.bfloat16)
    nt = (((1,), (1,)), ((), ()))
    both = lax.dot_general(wr_ref[...], x_hi, nt, preferred_element_type=jnp.float32)
    cross = lax.dot_general(wr_ref[0:N_EXPERTS, :], x_lo, nt, preferred_element_type=jnp.float32)
    logits_t = both[0:N_EXPERTS, :] + both[N_EXPERTS:, :] + cross + br_ref[...]
    cls_ref[0] = _route_class(logits_t)


def _mix_call(x, a, p, lw, seq, alpha):
    t, attn_dim = a.shape
    tm = TOK_TILE
    assert seq % tm == 0 and tm % POOL_HALO == 0
    pool_dim = p.shape[1]
    row = lambda i: (i, 0)
    halo_blocks = t // POOL_HALO
    per_tile = tm // POOL_HALO
    prev = lambda i: (jnp.maximum(i * per_tile - 1, 0), 0)
    nxt = lambda i: (jnp.minimum((i + 1) * per_tile, halo_blocks - 1), 0)
    weights = [lw["w_gate"], lw["b_gate"], lw["w_pool"], lw["b_pool"], lw["pool_scale"], lw["w_oa"],
               lw["w_op"], lw["w_out"], lw["b_out"], lw["ln1_g"], lw["ln1_b"], lw["w_router_t"],
               lw["b_router_t"]]
    return pl.pallas_call(
        functools.partial(_mix_kernel, seq=seq, alpha=alpha),
        grid=(t // tm,),
        in_specs=[_tok_spec(tm), pl.BlockSpec((tm, attn_dim), row),
                  pl.BlockSpec((tm, pool_dim), row), pl.BlockSpec((POOL_HALO, pool_dim), prev),
                  pl.BlockSpec((POOL_HALO, pool_dim), nxt)] + [_const_spec(w.shape) for w in weights],
        out_specs=[_tok_spec(tm), pl.BlockSpec((1, 1, tm), lambda i: (i, 0, 0))],
        out_shape=[jax.ShapeDtypeStruct((t * ROW_CHUNKS, LANES), jnp.float32),
                   jax.ShapeDtypeStruct((t // tm, 1, tm), jnp.int32)],
        scratch_shapes=[pltpu.VMEM((tm + 2 * POOL_HALO, pool_dim), jnp.float32)],
        compiler_params=_params(("parallel",)),
        name="mix",
    )(x, a, p, p, p, *weights)


def _rank_kernel(cls_ref, rank_ref, count_ref, carry_ref):
    tm = cls_ref.shape[2]

    @pl.when(pl.program_id(0) == 0)
    def _():
        carry_ref[...] = jnp.zeros_like(carry_ref)

    cls = cls_ref[0]
    onehot = lax.broadcasted_iota(jnp.int32, (CLASS_ROWS, tm), 0) == cls
    upper = (lax.broadcasted_iota(jnp.int32, (tm, tm), 0)
             <= lax.broadcasted_iota(jnp.int32, (tm, tm), 1)).astype(jnp.bfloat16)
    incl = jnp.dot(onehot.astype(jnp.bfloat16), upper, preferred_element_type=jnp.float32)
    before = incl - 1.0 + carry_ref[...]
    rank_ref[0] = jnp.sum(jnp.where(onehot, before, 0.0), axis=0, keepdims=True).astype(jnp.int32)
    carry_ref[...] = carry_ref[...] + incl[:, tm - 1:tm]
    count_ref[...] = carry_ref[...].astype(jnp.int32)


def _rank_call(cls):
    nt, _, tm = cls.shape
    return pl.pallas_call(
        _rank_kernel,
        grid=(nt,),
        in_specs=[pl.BlockSpec((1, 1, tm), lambda i: (i, 0, 0))],
        out_specs=[pl.BlockSpec((1, 1, tm), lambda i: (i, 0, 0)), _const_spec((CLASS_ROWS, 1))],
        out_shape=[jax.ShapeDtypeStruct((nt, 1, tm), jnp.int32),
                   jax.ShapeDtypeStruct((CLASS_ROWS, 1), jnp.int32)],
        scratch_shapes=[pltpu.VMEM((CLASS_ROWS, 1), jnp.float32)],
        compiler_params=_params(("arbitrary",)),
        name="rank",
    )(cls)


def _ffn_kernel(ea_ref, eb_ref, nvalid_ref, tok_ref, tok_next_ref, tok_prev_ref, x_hbm,
                w1a_ref, b1a_ref, w2a_ref, b2a_ref, w1b_ref, b1b_ref, w2b_ref, b2b_ref,
                wr_ref, br_ref, lng_ref, lnb_ref, out_hbm, xbuf, obuf, sem_in, sem_out,
                *, alpha, n_tok):
    blk = pl.program_id(0)
    last = pl.num_programs(0) - 1
    slot = blk % 2
    other = 1 - slot
    n_valid = nvalid_ref[blk]
    n_valid_prev = jnp.where(blk > 0, nvalid_ref[jnp.maximum(blk - 1, 0)], 0)
    rows = MOE_BLOCK
    tile = ROW_CHUNKS

    def gather_row(ids_ref, dst_slot, r):
        src = pl.multiple_of(ids_ref[0, 0, r] * tile, tile)
        return pltpu.make_async_copy(x_hbm.at[pl.ds(src, tile)], xbuf.at[dst_slot, pl.ds(r * tile, tile)],
                                     sem_in.at[dst_slot])

    def scatter_row(ids_ref, count, src_slot, r):
        token = jnp.where(r < count, ids_ref[0, 0, r], n_tok + r)
        dst = pl.multiple_of(token * tile, tile)
        return pltpu.make_async_copy(obuf.at[src_slot, pl.ds(r * tile, tile)], out_hbm.at[pl.ds(dst, tile)],
                                     sem_out.at[src_slot])

    def wait_gather(dst_slot):
        pltpu.make_async_copy(x_hbm.at[pl.ds(0, rows * tile)], xbuf.at[dst_slot], sem_in.at[dst_slot]).wait()

    def wait_scatter(src_slot):
        pltpu.make_async_copy(obuf.at[src_slot], out_hbm.at[pl.ds(0, rows * tile)], sem_out.at[src_slot]).wait()

    def start_all(make_copy):
        lax.fori_loop(0, rows, lambda r, c: (make_copy(r).start(), c)[1], 0, unroll=8)

    @pl.when(blk == 0)
    def _():
        obuf[1] = jnp.zeros(obuf.shape[1:], obuf.dtype)
        start_all(functools.partial(gather_row, tok_ref, 0))

    wait_gather(slot)

    def start_neighbours(r):
        gather_row(tok_next_ref, other, r).start()
        return scatter_row(tok_prev_ref, n_valid_prev, other, r)

    start_all(start_neighbours)

    @pl.when(n_valid > 0)
    def _():
        x = _load_rows(xbuf.at[slot], rows)
        xb = x.astype(jnp.bfloat16)

        def expert(w1_ref, b1_ref, w2_ref, b2_ref):
            h = jnp.dot(xb, w1_ref[0], preferred_element_type=jnp.float32) + b1_ref[0]
            h = 0.5 * h * (1.0 + lax.erf(h * (2.0 ** -0.5)))
            return jnp.dot(h.astype(jnp.bfloat16), w2_ref[0], preferred_element_type=jnp.float32) + b2_ref[0]

        ea = ea_ref[blk]
        eb = eb_ref[blk]
        w_diff = wr_ref[pl.ds(eb, 1), :] - wr_ref[pl.ds(ea, 1), :]
        delta = jnp.sum(x * w_diff, axis=-1, keepdims=True) + (br_ref[eb] - br_ref[ea])
        gate_a = 1.0 / (1.0 + jnp.exp(delta))
        gate_b = 1.0 / (1.0 + jnp.exp(-delta))
        y = gate_a * expert(w1a_ref, b1a_ref, w2a_ref, b2a_ref) \
            + gate_b * expert(w1b_ref, b1b_ref, w2b_ref, b2b_ref)
        _store_rows(obuf.at[slot], _layer_norm(alpha * x + y, lng_ref[...], lnb_ref[...]))

    wait_scatter(other)

    @pl.when(blk == last)
    def _():
        wait_gather(other)
        start_all(functools.partial(scatter_row, tok_ref, n_valid, slot))
        wait_scatter(slot)


def _ffn_call(x1, t, buf_tok, block_ea, block_eb, block_nvalid, lw, alpha):
    n_blocks = block_ea.shape[0]
    _, d, d_ff = lw["w1"].shape
    const = lambda shape: pl.BlockSpec(shape, lambda b, ea, eb, nv: (0,) * len(shape))
    by_a = lambda shape: pl.BlockSpec(shape, lambda b, ea, eb, nv: (ea[b], 0, 0))
    by_b = lambda shape: pl.BlockSpec(shape, lambda b, ea, eb, nv: (eb[b], 0, 0))
    ids = lambda shift: pl.BlockSpec(
        (1, 1, MOE_BLOCK), lambda b, ea, eb, nv: (jnp.clip(b + shift, 0, n_blocks - 1), 0, 0),
        memory_space=pltpu.SMEM)
    grid_spec = pltpu.PrefetchScalarGridSpec(
        num_scalar_prefetch=3,
        grid=(n_blocks,),
        in_specs=[
            ids(0), ids(1), ids(-1),
            pl.BlockSpec(memory_space=pl.ANY),
            by_a((1, d, d_ff)), by_a((1, 1, d_ff)), by_a((1, d_ff, d)), by_a((1, 1, d)),
            by_b((1, d, d_ff)), by_b((1, 1, d_ff)), by_b((1, d_ff, d)), by_b((1, 1, d)),
            const((N_EXPERTS, d)),
            pl.BlockSpec(memory_space=pltpu.SMEM),
            const((1, d)), const((1, d)),
        ],
        out_specs=pl.BlockSpec(memory_space=pl.ANY),
        scratch_shapes=[pltpu.VMEM((2, MOE_BLOCK * ROW_CHUNKS, LANES), jnp.float32),
                        pltpu.VMEM((2, MOE_BLOCK * ROW_CHUNKS, LANES), jnp.float32),
                        pltpu.SemaphoreType.DMA((2,)), pltpu.SemaphoreType.DMA((2,))],
    )
    return pl.pallas_call(
        functools.partial(_ffn_kernel, alpha=alpha, n_tok=t),
        grid_spec=grid_spec,
        out_shape=jax.ShapeDtypeStruct(((t + MOE_BLOCK) * ROW_CHUNKS, LANES), jnp.float32),
        compiler_params=_params(("arbitrary",)),
        name="ffn",
    )(block_ea, block_eb, block_nvalid, buf_tok, buf_tok, buf_tok, x1,
      lw["w1"], lw["b1"], lw["w2"], lw["b2"], lw["w1"], lw["b1"], lw["w2"], lw["b2"],
      lw["w_router_rows"], lw["b_router"], lw["ln2_g"], lw["ln2_b"])


def _dispatch_tables(cls, rank, counts, t):
    counts = counts[:N_CLASSES, 0]
    n_blocks = -(-(t + N_CLASSES * (MOE_BLOCK - 1)) // MOE_BLOCK)
    padded = (counts + MOE_BLOCK - 1) // MOE_BLOCK * MOE_BLOCK
    pad_ends = jnp.cumsum(padded)
    pad_starts = pad_ends - padded
    dest = pad_starts[cls.reshape(t)] + rank.reshape(t)
    buf_tok = jnp.zeros((n_blocks * MOE_BLOCK,), jnp.int32).at[dest].set(
        jnp.arange(t, dtype=jnp.int32), unique_indices=True, indices_are_sorted=False)
    block_start = jnp.arange(n_blocks, dtype=jnp.int32) * MOE_BLOCK
    block_cls = jnp.minimum(jnp.sum(pad_ends[None, :] <= block_start[:, None], axis=1), N_CLASSES - 1)
    block_cls = block_cls.astype(jnp.int32)
    n_valid = jnp.clip(counts[block_cls] - (block_start - pad_starts[block_cls]), 0, MOE_BLOCK)
    pair = jnp.asarray(PAIRS, jnp.int32)[block_cls % len(PAIRS)]
    group = block_cls // len(PAIRS)
    ea = group * EXPERTS_PER_GROUP + pair[:, 0]
    eb = group * EXPERTS_PER_GROUP + pair[:, 1]
    return (buf_tok.reshape(n_blocks, 1, MOE_BLOCK), ea.astype(jnp.int32), eb.astype(jnp.int32),
            n_valid.astype(jnp.int32))


def kernel(x_prompt, x_sample, ln_in_g, ln_in_b, w_in, b_in, rpb, w_pool, b_pool, pool_scale, w_oa, w_op,
           w_out, b_out, ln1_g, ln1_b, w_router, b_router, w1, b1, w2, b2, ln2_g, ln2_b):
    depth, d, _ = w_in.shape
    attn_dim = N_HEADS * HEAD_DIM
    pool_dim = w_op.shape[1]
    qkvp = 3 * attn_dim + pool_dim
    alpha = (2 * depth) ** 0.25
    bf = jnp.bfloat16
    f32 = jnp.float32

    wr_hi = w_router.T.astype(bf)
    wr_lo = (w_router.T - wr_hi.astype(f32)).astype(bf)
    shared = {
        "w_router_t": jnp.concatenate([wr_hi, wr_lo], axis=0),
        "b_router_t": b_router.reshape(N_EXPERTS, 1).astype(f32),
        "w_router_rows": w_router.T.astype(f32),
        "b_router": b_router.astype(f32),
    }
    layers = []
    for l in range(depth):
        lw = dict(shared)
        lw.update({
            "w_qkvp": w_in[l, :, :qkvp].astype(bf), "b_qkvp": b_in[l, :qkvp].reshape(1, -1),
            "w_gate": w_in[l, :, qkvp:].astype(bf), "b_gate": b_in[l, qkvp:].reshape(1, -1),
            "table": _bias_table(rpb[l]),
            "w_pool": w_pool[l].astype(bf), "b_pool": b_pool[l].reshape(1, -1),
            "pool_scale": pool_scale[l].reshape(1, -1),
            "w_oa": w_oa[l].astype(bf), "w_op": w_op[l].astype(bf), "w_out": w_out[l].astype(bf),
            "b_out": b_out[l].reshape(1, -1),
            "ln1_g": ln1_g[l].reshape(1, -1), "ln1_b": ln1_b[l].reshape(1, -1),
            "w1": w1[l].astype(bf), "b1": b1[l].reshape(N_EXPERTS, 1, -1),
            "w2": w2[l].astype(bf), "b2": b2[l].reshape(N_EXPERTS, 1, -1),
            "ln2_g": ln2_g[l].reshape(1, -1), "ln2_b": ln2_b[l].reshape(1, -1),
        })
        layers.append(lw)

    def run(x3):
        batch, seq, _ = x3.shape
        t = batch * seq
        x = _ln_call(x3.reshape(t, d), ln_in_g.reshape(1, -1), ln_in_b.reshape(1, -1))
        for lw in layers:
            q, k, v, p = _proj_call(x, t, lw["w_qkvp"], lw["b_qkvp"], attn_dim, pool_dim)
            a = _attn_call(q, k, v, lw["table"], batch, seq)
            x1, cls = _mix_call(x, a, p, lw, seq, alpha)
            rank, counts = _rank_call(cls)
            buf_tok, ea, eb, n_valid = _dispatch_tables(cls, rank, counts, t)
            x = _ffn_call(x1, t, buf_tok, ea, eb, n_valid, lw, alpha)
        return _unpack_call(x, t).reshape(batch, seq, d)

    return (run(x_prompt), run(x_sample))
```

```python
import functools

import jax
import jax.numpy as jnp
from jax import lax
from jax.experimental import pallas as pl
from jax.experimental.pallas import tpu as pltpu

GRID_W = 64
N_HEADS = 8
HEAD_DIM = 64
WIN_ROWS = 8
WIN_COLS = 16
POOL_WINDOWS = (2, 4, 8, 16)
N_EXPERTS = 16
N_GROUPS = 4
EXPERTS_PER_GROUP = 4
PAIRS = ((0, 1), (0, 2), (0, 3), (1, 2), (1, 3), (2, 3))
N_CLASSES = N_GROUPS * len(PAIRS)
CLASS_ROWS = 32
MOE_BLOCK = 256
LN_EPS = 1e-5
NEG = -1e30

LANES = 128
ROW_CHUNKS = 8
LOG2_E = 1.4426950408889634
Q_ROWS = 4
KV_ROWS = 12
KV_BLK_ROWS = 4
TOK_TILE = 512
MIX_SUBTILES = 1
POOL_HALO = 8
VMEM_LIMIT = 56 * 1024 * 1024


def _layer_norm(x, g, b):
    mu = jnp.mean(x, axis=-1, keepdims=True)
    xc = x - mu
    var = jnp.mean(xc * xc, axis=-1, keepdims=True)
    return xc * lax.rsqrt(var + LN_EPS) * g + b


def _params(semantics):
    return pltpu.CompilerParams(dimension_semantics=semantics, vmem_limit_bytes=VMEM_LIMIT)


def _const_spec(shape):
    nd = len(shape)
    return pl.BlockSpec(shape, lambda *_: (0,) * nd)


def _load_rows(ref, n_tok, tok0=0):
    return jnp.concatenate([ref[pl.ds(tok0 * ROW_CHUNKS + s, n_tok, stride=ROW_CHUNKS), :]
                            for s in range(ROW_CHUNKS)], axis=1)


def _store_rows(ref, value, tok0=0):
    n_tok = value.shape[0]
    for s in range(ROW_CHUNKS):
        ref[pl.ds(tok0 * ROW_CHUNKS + s, n_tok, stride=ROW_CHUNKS), :] = value[:, s * LANES:(s + 1) * LANES]


def _tok_spec(tm):
    return pl.BlockSpec((tm * ROW_CHUNKS, LANES), lambda i: (i, 0))


def _ln_kernel(x_ref, g_ref, b_ref, o_ref):
    _store_rows(o_ref, _layer_norm(x_ref[...], g_ref[...], b_ref[...]))


def _ln_call(x, g, b):
    t, d = x.shape
    assert d == ROW_CHUNKS * LANES
    tm = TOK_TILE
    return pl.pallas_call(
        _ln_kernel,
        grid=(t // tm,),
        in_specs=[pl.BlockSpec((tm, d), lambda i: (i, 0)), _const_spec((1, d)), _const_spec((1, d))],
        out_specs=_tok_spec(tm),
        out_shape=jax.ShapeDtypeStruct((t * ROW_CHUNKS, LANES), jnp.float32),
        compiler_params=_params(("parallel",)),
        name="ln_in",
    )(x, g, b)


def _unpack_kernel(x_ref, o_ref):
    o_ref[...] = _load_rows(x_ref, o_ref.shape[0])


def _unpack_call(x, t):
    tm = TOK_TILE
    d = ROW_CHUNKS * LANES
    return pl.pallas_call(
        _unpack_kernel,
        grid=(t // tm,),
        in_specs=[_tok_spec(tm)],
        out_specs=pl.BlockSpec((tm, d), lambda i: (i, 0)),
        out_shape=jax.ShapeDtypeStruct((t, d), jnp.float32),
        compiler_params=_params(("parallel",)),
        name="unpack",
    )(x)


def _proj_kernel(x_ref, w_ref, b_ref, wvt_ref, bvt_ref, q_ref, k_ref, vt_ref, p_ref, *, attn_dim, scale):
    xb = _load_rows(x_ref, q_ref.shape[0]).astype(jnp.bfloat16)
    acc = jnp.dot(xb, w_ref[...], preferred_element_type=jnp.float32) + b_ref[...]
    q_ref[...] = (acc[:, :attn_dim] * scale).astype(jnp.bfloat16)
    k_ref[...] = acc[:, attn_dim:2 * attn_dim].astype(jnp.bfloat16)
    p_ref[...] = acc[:, 2 * attn_dim:]
    vt = lax.dot_general(wvt_ref[...], xb, (((1,), (1,)), ((), ())), preferred_element_type=jnp.float32)
    vt_ref[...] = (vt + bvt_ref[...]).astype(jnp.bfloat16)


def _proj_call(x, t, lw, attn_dim, pool_dim):
    w, b, wvt, bvt = lw["w_qkp"], lw["b_qkp"], lw["w_v_t"], lw["b_v_t"]
    d, n = w.shape
    tm = TOK_TILE
    row = lambda i: (i, 0)
    return pl.pallas_call(
        functools.partial(_proj_kernel, attn_dim=attn_dim, scale=HEAD_DIM ** -0.5 * LOG2_E),
        grid=(t // tm,),
        in_specs=[_tok_spec(tm), _const_spec((d, n)), _const_spec((1, n)), _const_spec(wvt.shape),
                  _const_spec(bvt.shape)],
        out_specs=[pl.BlockSpec((tm, attn_dim), row), pl.BlockSpec((tm, attn_dim), row),
                   pl.BlockSpec((attn_dim, tm), lambda i: (0, i)), pl.BlockSpec((tm, pool_dim), row)],
        out_shape=[jax.ShapeDtypeStruct((t, attn_dim), jnp.bfloat16),
                   jax.ShapeDtypeStruct((t, attn_dim), jnp.bfloat16),
                   jax.ShapeDtypeStruct((attn_dim, t), jnp.bfloat16),
                   jax.ShapeDtypeStruct((t, pool_dim), jnp.float32)],
        compiler_params=_params(("parallel",)),
        name="proj",
    )(x, w, b, wvt, bvt)


def _attn_kernel(q_ref, k0, k1, k2, vt0, vt1, vt2, tab_ref, o_ref):
    q = q_ref[...]
    k = jnp.concatenate([k0[...], k1[...], k2[...]], axis=0)
    vt = jnp.concatenate([vt0[...], vt1[...], vt2[...]], axis=1)
    lane_head = lax.broadcasted_iota(jnp.int32, (1, LANES), 1) // HEAD_DIM
    heads_per_vreg = LANES // HEAD_DIM

    def scores(h):
        group = slice(h // heads_per_vreg * LANES, (h // heads_per_vreg + 1) * LANES)
        q_h = jnp.where(lane_head == h % heads_per_vreg, q[:, group], jnp.zeros((), q.dtype))
        s = lax.dot_general(k[:, group], q_h, (((1,), (1,)), ((), ())),
                            preferred_element_type=jnp.float32)
        return s + tab_ref[0, h]

    s_next = scores(0)
    for h in range(N_HEADS):
        s = s_next
        if h + 1 < N_HEADS:
            s_next = scores(h + 1)
        m = jnp.max(s, axis=0, keepdims=True)
        e = jnp.exp2(s - m)
        denom = jnp.sum(e, axis=0, keepdims=True)
        rows_h = slice(h * HEAD_DIM, (h + 1) * HEAD_DIM)
        o = jnp.dot(vt[rows_h, :], e.astype(jnp.bfloat16), preferred_element_type=jnp.float32)
        o_ref[rows_h, :] = (o / denom).astype(o_ref.dtype)


def _attn_call(q, k, vt, table, batch, seq):
    t, attn_dim = q.shape
    rows = seq // GRID_W
    assert rows % Q_ROWS == 0 and rows >= WIN_ROWS
    assert Q_ROWS == KV_BLK_ROWS == WIN_ROWS // 2 and KV_ROWS == 3 * KV_BLK_ROWS
    nq = Q_ROWS * GRID_W
    q_blocks = seq // nq
    kv_block = lambda i, b, j: b * q_blocks + jnp.clip(i - 1 + j, 0, q_blocks - 1)

    def table_index(i, b):
        return (jnp.where(i == 0, 0, jnp.where(i == q_blocks - 1, 2, 1)), 0, 0, 0)

    n_kv = KV_ROWS // KV_BLK_ROWS
    k_specs = [pl.BlockSpec((nq, attn_dim), functools.partial(lambda i, b, j: (kv_block(i, b, j), 0), j=j))
               for j in range(n_kv)]
    vt_specs = [pl.BlockSpec((attn_dim, nq), functools.partial(lambda i, b, j: (0, kv_block(i, b, j)), j=j))
                for j in range(n_kv)]
    return pl.pallas_call(
        _attn_kernel,
        grid=(q_blocks, batch),
        in_specs=[pl.BlockSpec((nq, attn_dim), lambda i, b: (b * q_blocks + i, 0))] + k_specs + vt_specs
        + [pl.BlockSpec((1,) + table.shape[1:], table_index)],
        out_specs=pl.BlockSpec((attn_dim, nq), lambda i, b: (0, b * q_blocks + i)),
        out_shape=jax.ShapeDtypeStruct((attn_dim, t), jnp.bfloat16),
        compiler_params=_params(("parallel", "parallel")),
        name="attn",
    )(q, k, k, k, vt, vt, vt, table)


def _bias_table(rpb):
    n_dr = 2 * WIN_ROWS - 1
    n_dc = 2 * WIN_COLS - 1
    n_heads = rpb.shape[0]
    c = jnp.arange(GRID_W)
    col_start = jnp.clip(c - WIN_COLS // 2, 0, GRID_W - WIN_COLS)
    col_ok = (c[None, :] >= col_start[:, None]) & (c[None, :] < col_start[:, None] + WIN_COLS)
    dc = c[None, :] - c[:, None] + (WIN_COLS - 1)
    onehot = ((dc[None] == jnp.arange(n_dc)[:, None, None]) & col_ok[None]).astype(jnp.float32)
    bank = jnp.einsum("hrd,dck->hrck", rpb.astype(jnp.float32) * LOG2_E, onehot,
                      precision=lax.Precision.HIGHEST)
    lead = (KV_ROWS - Q_ROWS) // 2
    i = jnp.arange(Q_ROWS)[:, None]
    j = jnp.arange(KV_ROWS)[None, :]
    row_ok = jnp.stack([(j >= lead) & (j < lead + WIN_ROWS) & (i >= 0),
                        (j - i >= 0) & (j - i < WIN_ROWS),
                        (j < WIN_ROWS) & (i >= 0)])
    first = [WIN_ROWS - 1 - lead - qi for qi in range(Q_ROWS)]
    assert min(first) >= 0 and max(first) + KV_ROWS <= n_dr
    tiles = jnp.stack([bank[:, f:f + KV_ROWS] for f in first], axis=1)
    ok = row_ok[:, None, :, :, None, None] & col_ok[None, None, None, None]
    tab = jnp.where(ok, tiles[None], NEG)
    tab = jnp.transpose(tab, (0, 1, 3, 5, 2, 4))
    return tab.reshape(3, n_heads, KV_ROWS * GRID_W, Q_ROWS * GRID_W)


def _route_class(logits_t):
    m = jnp.max(logits_t, axis=0, keepdims=True)
    e = jnp.exp(logits_t - m)
    probs = e / jnp.sum(e, axis=0, keepdims=True)
    p = [probs[r:r + 1, :] for r in range(N_EXPERTS)]

    def top2_sum(a, b, c, d):
        hi1, lo1 = jnp.maximum(a, b), jnp.minimum(a, b)
        hi2, lo2 = jnp.maximum(c, d), jnp.minimum(c, d)
        return jnp.maximum(hi1, hi2) + jnp.maximum(jnp.minimum(hi1, hi2), jnp.maximum(lo1, lo2))

    score = [top2_sum(*p[4 * g:4 * g + 4]) for g in range(N_GROUPS)]
    best_g = jnp.zeros_like(score[0], dtype=jnp.int32)
    best_s = score[0]
    for g in range(1, N_GROUPS):
        upd = score[g] > best_s
        best_g = jnp.where(upd, g, best_g)
        best_s = jnp.where(upd, score[g], best_s)

    vals = []
    for r in range(EXPERTS_PER_GROUP):
        val = p[r]
        for g in range(1, N_GROUPS):
            val = jnp.where(best_g == g, p[4 * g + r], val)
        vals.append(val)

    def argmax4(vs):
        idx = jnp.zeros_like(best_g)
        top = vs[0]
        for r in range(1, EXPERTS_PER_GROUP):
            upd = vs[r] > top
            idx = jnp.where(upd, r, idx)
            top = jnp.where(upd, vs[r], top)
        return idx

    i1 = argmax4(vals)
    i2 = argmax4([jnp.where(i1 == r, -1.0, vals[r]) for r in range(EXPERTS_PER_GROUP)])
    lo = jnp.minimum(i1, i2)
    hi = jnp.maximum(i1, i2)
    pair = jnp.zeros_like(lo)
    for n, (a, b) in enumerate(PAIRS):
        pair = jnp.where((lo == a) & (hi == b), n, pair)
    return best_g * len(PAIRS) + pair


def _mix_kernel(x_ref, a_ref, p_ref, pprev_ref, pnext_ref, wg_ref, bg_ref, wpool_ref, bpool_ref,
                pscale_ref, woa_ref, wop_ref, wout_ref, bout_ref, lng_ref, lnb_ref, wr_ref, br_ref,
                x1_ref, cls_ref, halo_ref, *, seq, alpha):
    tm = a_ref.shape[1]
    ts = tm // MIX_SUBTILES
    d = wout_ref.shape[1]
    pool_dim = p_ref.shape[1]
    group_dim = pool_dim // len(POOL_WINDOWS)
    tiles_per_seq = seq // tm
    tile_in_seq = pl.program_id(0) % tiles_per_seq
    at_start = tile_in_seq == 0
    at_end = tile_in_seq == tiles_per_seq - 1

    halo_ref[0:POOL_HALO, :] = jnp.where(at_start, 0.0, pprev_ref[...])
    halo_ref[POOL_HALO:POOL_HALO + tm, :] = p_ref[...]
    halo_ref[POOL_HALO + tm:, :] = jnp.where(at_end, 0.0, pnext_ref[...])

    def projections(r0):
        x = _load_rows(x_ref, ts, r0)
        half_z = jnp.dot(x.astype(jnp.bfloat16), wg_ref[...], preferred_element_type=jnp.float32) + bg_ref[...]
        attn_out = lax.dot_general(a_ref[:, r0:r0 + ts], woa_ref[...], (((0,), (0,)), ((), ())),
                                   preferred_element_type=jnp.float32)
        return x, half_z, attn_out

    def window_diffs(r0):
        pos = tile_in_seq * tm + r0 + lax.broadcasted_iota(jnp.int32, (ts, 1), 0)
        base = POOL_HALO + r0
        diffs = []
        for g, w in enumerate(POOL_WINDOWS):
            lanes = slice(g * group_dim, (g + 1) * group_dim)
            total = halo_ref[base - w // 2:base - w // 2 + ts, lanes]
            for off in range(-w // 2 + 1, w // 2):
                total = total + halo_ref[base + off:base + off + ts, lanes]
            count = jnp.clip(pos + w // 2, 0, seq) - jnp.clip(pos - w // 2, 0, seq)
            diff = total / count.astype(jnp.float32) - halo_ref[base:base + ts, lanes]
            diffs.append(diff.astype(jnp.bfloat16))
        return diffs

    def mixer(diffs, half_z, attn_out):
        gates2 = jnp.tanh(half_z) + 1.0
        pooled = []
        for g, diff in enumerate(diffs):
            lanes = slice(g * group_dim, (g + 1) * group_dim)
            y = jnp.dot(diff, wpool_ref[g], preferred_element_type=jnp.float32)
            pooled.append((y + bpool_ref[:, lanes]) * pscale_ref[:, lanes])
        pm = jnp.concatenate(pooled, axis=1)
        pool_out = jnp.dot(pm.astype(jnp.bfloat16), wop_ref[...], preferred_element_type=jnp.float32)
        mix2 = gates2[:, :d] * attn_out + gates2[:, d:] * pool_out
        return jnp.dot(mix2.astype(jnp.bfloat16), wout_ref[...], preferred_element_type=jnp.float32) + bout_ref[...]

    def finish(r0, x, y):
        x1 = _layer_norm(alpha * x + y, lng_ref[...], lnb_ref[...])
        _store_rows(x1_ref, x1, r0)
        x_hi = x1.astype(jnp.bfloat16)
        x_lo = (x1 - x_hi.astype(jnp.float32)).astype(jnp.bfloat16)
        nt = (((1,), (1,)), ((), ()))
        both = lax.dot_general(wr_ref[...], x_hi, nt, preferred_element_type=jnp.float32)
        cross = lax.dot_general(wr_ref[0:N_EXPERTS, :], x_lo, nt, preferred_element_type=jnp.float32)
        logits_t = both[0:N_EXPERTS, :] + both[N_EXPERTS:, :] + cross + br_ref[...]
        cls_ref[0, :, r0:r0 + ts] = _route_class(logits_t)

    starts = [n * ts for n in range(MIX_SUBTILES)]
    diffs = [window_diffs(r0) for r0 in starts]
    staged = [projections(r0) for r0 in starts]
    mixed = [(x, mixer(df, half_z, attn_out)) for df, (x, half_z, attn_out) in zip(diffs, staged)]
    for r0, (x, y) in zip(starts, mixed):
        finish(r0, x, y)


def _mix_call(x, a, p, lw, seq, alpha):
    attn_dim, t = a.shape
    tm = TOK_TILE
    assert seq % tm == 0 and tm % POOL_HALO == 0
    pool_dim = p.shape[1]
    row = lambda i: (i, 0)
    halo_blocks = t // POOL_HALO
    per_tile = tm // POOL_HALO
    prev = lambda i: (jnp.maximum(i * per_tile - 1, 0), 0)
    nxt = lambda i: (jnp.minimum((i + 1) * per_tile, halo_blocks - 1), 0)
    weights = [lw["w_gate"], lw["b_gate"], lw["w_pool"], lw["b_pool"], lw["pool_scale"], lw["w_oa"],
               lw["w_op"], lw["w_out"], lw["b_out"], lw["ln1_g"], lw["ln1_b"], lw["w_router_t"],
               lw["b_router_t"]]
    return pl.pallas_call(
        functools.partial(_mix_kernel, seq=seq, alpha=alpha),
        grid=(t // tm,),
        in_specs=[_tok_spec(tm), pl.BlockSpec((attn_dim, tm), lambda i: (0, i)),
                  pl.BlockSpec((tm, pool_dim), row), pl.BlockSpec((POOL_HALO, pool_dim), prev),
                  pl.BlockSpec((POOL_HALO, pool_dim), nxt)] + [_const_spec(w.shape) for w in weights],
        out_specs=[_tok_spec(tm), pl.BlockSpec((1, 1, tm), lambda i: (i, 0, 0))],
        out_shape=[jax.ShapeDtypeStruct((t * ROW_CHUNKS, LANES), jnp.float32),
                   jax.ShapeDtypeStruct((t // tm, 1, tm), jnp.int32)],
        scratch_shapes=[pltpu.VMEM((tm + 2 * POOL_HALO, pool_dim), jnp.float32)],
        compiler_params=_params(("parallel",)),
        name="mix",
    )(x, a, p, p, p, *weights)


def _rank_kernel(cls_ref, rank_ref, count_ref, carry_ref):
    tm = cls_ref.shape[2]

    @pl.when(pl.program_id(0) == 0)
    def _():
        carry_ref[...] = jnp.zeros_like(carry_ref)

    cls = cls_ref[0]
    onehot = lax.broadcasted_iota(jnp.int32, (CLASS_ROWS, tm), 0) == cls
    upper = (lax.broadcasted_iota(jnp.int32, (tm, tm), 0)
             <= lax.broadcasted_iota(jnp.int32, (tm, tm), 1)).astype(jnp.bfloat16)
    incl = jnp.dot(onehot.astype(jnp.bfloat16), upper, preferred_element_type=jnp.float32)
    before = incl - 1.0 + carry_ref[...]
    rank_ref[0] = jnp.sum(jnp.where(onehot, before, 0.0), axis=0, keepdims=True).astype(jnp.int32)
    carry_ref[...] = carry_ref[...] + incl[:, tm - 1:tm]
    count_ref[...] = carry_ref[...].astype(jnp.int32)


def _rank_call(cls):
    nt, _, tm = cls.shape
    return pl.pallas_call(
        _rank_kernel,
        grid=(nt,),
        in_specs=[pl.BlockSpec((1, 1, tm), lambda i: (i, 0, 0))],
        out_specs=[pl.BlockSpec((1, 1, tm), lambda i: (i, 0, 0)), _const_spec((CLASS_ROWS, 1))],
        out_shape=[jax.ShapeDtypeStruct((nt, 1, tm), jnp.int32),
                   jax.ShapeDtypeStruct((CLASS_ROWS, 1), jnp.int32)],
        scratch_shapes=[pltpu.VMEM((CLASS_ROWS, 1), jnp.float32)],
        compiler_params=_params(("arbitrary",)),
        name="rank",
    )(cls)


def _ffn_kernel(ea_ref, eb_ref, nvalid_ref, tok_ref, tok_next_ref, tok_prev_ref, x_hbm,
                w1a_ref, b1a_ref, w2a_ref, b2a_ref, w1b_ref, b1b_ref, w2b_ref, b2b_ref,
                wr_ref, br_ref, lng_ref, lnb_ref, out_hbm, xbuf, obuf, sem_in, sem_out,
                *, alpha, n_tok):
    blk = pl.program_id(0)
    last = pl.num_programs(0) - 1
    slot = blk % 2
    other = 1 - slot
    n_valid = nvalid_ref[blk]
    n_valid_prev = jnp.where(blk > 0, nvalid_ref[jnp.maximum(blk - 1, 0)], 0)
    rows = MOE_BLOCK
    tile = ROW_CHUNKS

    def gather_row(ids_ref, dst_slot, r):
        src = pl.multiple_of(ids_ref[0, 0, r] * tile, tile)
        return pltpu.make_async_copy(x_hbm.at[pl.ds(src, tile)], xbuf.at[dst_slot, pl.ds(r * tile, tile)],
                                     sem_in.at[dst_slot])

    def scatter_row(ids_ref, count, src_slot, r):
        token = jnp.where(r < count, ids_ref[0, 0, r], n_tok + r)
        dst = pl.multiple_of(token * tile, tile)
        return pltpu.make_async_copy(obuf.at[src_slot, pl.ds(r * tile, tile)], out_hbm.at[pl.ds(dst, tile)],
                                     sem_out.at[src_slot])

    def wait_gather(dst_slot):
        pltpu.make_async_copy(x_hbm.at[pl.ds(0, rows * tile)], xbuf.at[dst_slot], sem_in.at[dst_slot]).wait()

    def wait_scatter(src_slot):
        pltpu.make_async_copy(obuf.at[src_slot], out_hbm.at[pl.ds(0, rows * tile)], sem_out.at[src_slot]).wait()

    def start_all(make_copy):
        lax.fori_loop(0, rows, lambda r, c: (make_copy(r).start(), c)[1], 0, unroll=8)

    @pl.when(blk == 0)
    def _():
        obuf[1] = jnp.zeros(obuf.shape[1:], obuf.dtype)
        start_all(functools.partial(gather_row, tok_ref, 0))

    wait_gather(slot)

    def start_neighbours(r):
        gather_row(tok_next_ref, other, r).start()
        return scatter_row(tok_prev_ref, n_valid_prev, other, r)

    start_all(start_neighbours)

    @pl.when(n_valid > 0)
    def _():
        x = _load_rows(xbuf.at[slot], rows)
        xb = x.astype(jnp.bfloat16)

        def up(w1_ref, b1_ref):
            return jnp.dot(xb, w1_ref[0], preferred_element_type=jnp.float32) + b1_ref[0]

        def down(h, w2_ref, b2_ref):
            h = 0.5 * h * (1.0 + lax.erf(h * (2.0 ** -0.5)))
            return jnp.dot(h.astype(jnp.bfloat16), w2_ref[0], preferred_element_type=jnp.float32) + b2_ref[0]

        h_a = up(w1a_ref, b1a_ref)
        h_b = up(w1b_ref, b1b_ref)

        ea = ea_ref[blk]
        eb = eb_ref[blk]
        w_diff = wr_ref[pl.ds(eb, 1), :] - wr_ref[pl.ds(ea, 1), :]
        delta = jnp.sum(x * w_diff, axis=-1, keepdims=True) + (br_ref[eb] - br_ref[ea])
        gate_a = 1.0 / (1.0 + jnp.exp(delta))
        gate_b = 1.0 / (1.0 + jnp.exp(-delta))
        y = gate_a * down(h_a, w2a_ref, b2a_ref) + gate_b * down(h_b, w2b_ref, b2b_ref)
        _store_rows(obuf.at[slot], _layer_norm(alpha * x + y, lng_ref[...], lnb_ref[...]))

    wait_scatter(other)

    @pl.when(blk == last)
    def _():
        wait_gather(other)
        start_all(functools.partial(scatter_row, tok_ref, n_valid, slot))
        wait_scatter(slot)


def _ffn_call(x1, t, buf_tok, block_ea, block_eb, block_nvalid, lw, alpha):
    n_blocks = block_ea.shape[0]
    _, d, d_ff = lw["w1"].shape
    const = lambda shape: pl.BlockSpec(shape, lambda b, ea, eb, nv: (0,) * len(shape))
    by_a = lambda shape: pl.BlockSpec(shape, lambda b, ea, eb, nv: (ea[b], 0, 0))
    by_b = lambda shape: pl.BlockSpec(shape, lambda b, ea, eb, nv: (eb[b], 0, 0))
    ids = lambda shift: pl.BlockSpec(
        (1, 1, MOE_BLOCK), lambda b, ea, eb, nv: (jnp.clip(b + shift, 0, n_blocks - 1), 0, 0),
        memory_space=pltpu.SMEM)
    grid_spec = pltpu.PrefetchScalarGridSpec(
        num_scalar_prefetch=3,
        grid=(n_blocks,),
        in_specs=[
            ids(0), ids(1), ids(-1),
            pl.BlockSpec(memory_space=pl.ANY),
            by_a((1, d, d_ff)), by_a((1, 1, d_ff)), by_a((1, d_ff, d)), by_a((1, 1, d)),
            by_b((1, d, d_ff)), by_b((1, 1, d_ff)), by_b((1, d_ff, d)), by_b((1, 1, d)),
            const((N_EXPERTS, d)),
            pl.BlockSpec(memory_space=pltpu.SMEM),
            const((1, d)), const((1, d)),
        ],
        out_specs=pl.BlockSpec(memory_space=pl.ANY),
        scratch_shapes=[pltpu.VMEM((2, MOE_BLOCK * ROW_CHUNKS, LANES), jnp.float32),
                        pltpu.VMEM((2, MOE_BLOCK * ROW_CHUNKS, LANES), jnp.float32),
                        pltpu.SemaphoreType.DMA((2,)), pltpu.SemaphoreType.DMA((2,))],
    )
    return pl.pallas_call(
        functools.partial(_ffn_kernel, alpha=alpha, n_tok=t),
        grid_spec=grid_spec,
        out_shape=jax.ShapeDtypeStruct(((t + MOE_BLOCK) * ROW_CHUNKS, LANES), jnp.float32),
        compiler_params=_params(("arbitrary",)),
        name="ffn",
    )(block_ea, block_eb, block_nvalid, buf_tok, buf_tok, buf_tok, x1,
      lw["w1"], lw["b1"], lw["w2"], lw["b2"], lw["w1"], lw["b1"], lw["w2"], lw["b2"],
      lw["w_router_rows"], lw["b_router"], lw["ln2_g"], lw["ln2_b"])


def _dispatch_tables(cls, rank, counts, t):
    counts = counts[:N_CLASSES, 0]
    n_blocks = -(-(t + N_CLASSES * (MOE_BLOCK - 1)) // MOE_BLOCK)
    padded = (counts + MOE_BLOCK - 1) // MOE_BLOCK * MOE_BLOCK
    pad_ends = jnp.cumsum(padded)
    pad_starts = pad_ends - padded
    dest = pad_starts[cls.reshape(t)] + rank.reshape(t)
    buf_tok = jnp.zeros((n_blocks * MOE_BLOCK,), jnp.int32).at[dest].set(
        jnp.arange(t, dtype=jnp.int32), unique_indices=True, indices_are_sorted=False)
    block_start = jnp.arange(n_blocks, dtype=jnp.int32) * MOE_BLOCK
    block_cls = jnp.minimum(jnp.sum(pad_ends[None, :] <= block_start[:, None], axis=1), N_CLASSES - 1)
    block_cls = block_cls.astype(jnp.int32)
    n_valid = jnp.clip(counts[block_cls] - (block_start - pad_starts[block_cls]), 0, MOE_BLOCK)
    pair = jnp.asarray(PAIRS, jnp.int32)[block_cls % len(PAIRS)]
    group = block_cls // len(PAIRS)
    ea = group * EXPERTS_PER_GROUP + pair[:, 0]
    eb = group * EXPERTS_PER_GROUP + pair[:, 1]
    return (buf_tok.reshape(n_blocks, 1, MOE_BLOCK), ea.astype(jnp.int32), eb.astype(jnp.int32),
            n_valid.astype(jnp.int32))


def kernel(x_prompt, x_sample, ln_in_g, ln_in_b, w_in, b_in, rpb, w_pool, b_pool, pool_scale, w_oa, w_op,
           w_out, b_out, ln1_g, ln1_b, w_router, b_router, w1, b1, w2, b2, ln2_g, ln2_b):
    depth, d, _ = w_in.shape
    attn_dim = N_HEADS * HEAD_DIM
    pool_dim = w_op.shape[1]
    qkvp = 3 * attn_dim + pool_dim
    alpha = (2 * depth) ** 0.25
    bf = jnp.bfloat16
    f32 = jnp.float32

    wr_hi = w_router.T.astype(bf)
    wr_lo = (w_router.T - wr_hi.astype(f32)).astype(bf)
    shared = {
        "w_router_t": jnp.concatenate([wr_hi, wr_lo], axis=0),
        "b_router_t": b_router.reshape(N_EXPERTS, 1).astype(f32),
        "w_router_rows": w_router.T.astype(f32),
        "b_router": b_router.astype(f32),
    }
    layers = []
    for l in range(depth):
        lw = dict(shared)
        lw.update({
            "w_qkp": jnp.concatenate([w_in[l, :, :2 * attn_dim], w_in[l, :, 3 * attn_dim:qkvp]],
                                     axis=1).astype(bf),
            "b_qkp": jnp.concatenate([b_in[l, :2 * attn_dim], b_in[l, 3 * attn_dim:qkvp]]).reshape(1, -1),
            "w_v_t": w_in[l, :, 2 * attn_dim:3 * attn_dim].T.astype(bf),
            "b_v_t": b_in[l, 2 * attn_dim:3 * attn_dim].reshape(-1, 1),
            "w_gate": (0.5 * w_in[l, :, qkvp:]).astype(bf), "b_gate": 0.5 * b_in[l, qkvp:].reshape(1, -1),
            "table": _bias_table(rpb[l]),
            "w_pool": w_pool[l].astype(bf), "b_pool": b_pool[l].reshape(1, -1),
            "pool_scale": pool_scale[l].reshape(1, -1),
            "w_oa": w_oa[l].astype(bf), "w_op": w_op[l].astype(bf), "w_out": (0.5 * w_out[l]).astype(bf),
            "b_out": b_out[l].reshape(1, -1),
            "ln1_g": ln1_g[l].reshape(1, -1), "ln1_b": ln1_b[l].reshape(1, -1),
            "w1": w1[l].astype(bf), "b1": b1[l].reshape(N_EXPERTS, 1, -1),
            "w2": w2[l].astype(bf), "b2": b2[l].reshape(N_EXPERTS, 1, -1),
            "ln2_g": ln2_g[l].reshape(1, -1), "ln2_b": ln2_b[l].reshape(1, -1),
        })
        layers.append(lw)

    def run(x3):
        batch, seq, _ = x3.shape
        t = batch * seq
        x = _ln_call(x3.reshape(t, d), ln_in_g.reshape(1, -1), ln_in_b.reshape(1, -1))
        for lw in layers:
            q, k, vt, p = _proj_call(x, t, lw, attn_dim, pool_dim)
            a = _attn_call(q, k, vt, lw["table"], batch, seq)
            x1, cls = _mix_call(x, a, p, lw, seq, alpha)
            rank, counts = _rank_call(cls)
            buf_tok, ea, eb, n_valid = _dispatch_tables(cls, rank, counts, t)
            x = _ffn_call(x1, t, buf_tok, ea, eb, n_valid, lw, alpha)
        return _unpack_call(x, t).reshape(batch, seq, d)

    return (run(x_prompt), run(x_sample))
```

```python
import functools

import jax
import jax.numpy as jnp
from jax import lax
from jax.experimental import pallas as pl
from jax.experimental.pallas import tpu as pltpu

GRID_W = 64
N_HEADS = 8
HEAD_DIM = 64
WIN_ROWS = 8
WIN_COLS = 16
POOL_WINDOWS = (2, 4, 8, 16)
N_EXPERTS = 16
N_GROUPS = 4
EXPERTS_PER_GROUP = 4
PAIRS = ((0, 1), (0, 2), (0, 3), (1, 2), (1, 3), (2, 3))
N_CLASSES = N_GROUPS * len(PAIRS)
CLASS_ROWS = 32
MOE_BLOCK = 256
LN_EPS = 1e-5
NEG = -1e30

LANES = 128
ROW_CHUNKS = 8
LOG2_E = 1.4426950408889634
Q_ROWS = 4
KV_ROWS = 12
KV_BLK_ROWS = 4
TOK_TILE = 512
MIX_SUBTILES = 1
POOL_HALO = 8
VMEM_LIMIT = 56 * 1024 * 1024


def _layer_norm(x, g, b):
    mu = jnp.mean(x, axis=-1, keepdims=True)
    xc = x - mu
    var = jnp.mean(xc * xc, axis=-1, keepdims=True)
    return xc * lax.rsqrt(var + LN_EPS) * g + b


def _params(semantics):
    return pltpu.CompilerParams(dimension_semantics=semantics, vmem_limit_bytes=VMEM_LIMIT)


def _const_spec(shape):
    nd = len(shape)
    return pl.BlockSpec(shape, lambda *_: (0,) * nd)


def _load_rows(ref, n_tok, tok0=0):
    return jnp.concatenate([ref[pl.ds(tok0 * ROW_CHUNKS + s, n_tok, stride=ROW_CHUNKS), :]
                            for s in range(ROW_CHUNKS)], axis=1)


def _store_rows(ref, value, tok0=0):
    n_tok = value.shape[0]
    for s in range(ROW_CHUNKS):
        ref[pl.ds(tok0 * ROW_CHUNKS + s, n_tok, stride=ROW_CHUNKS), :] = value[:, s * LANES:(s + 1) * LANES]


def _tok_spec(tm):
    return pl.BlockSpec((tm * ROW_CHUNKS, LANES), lambda i: (i, 0))


def _ln_kernel(x_ref, g_ref, b_ref, o_ref):
    _store_rows(o_ref, _layer_norm(x_ref[...], g_ref[...], b_ref[...]))


def _ln_call(x, g, b):
    t, d = x.shape
    assert d == ROW_CHUNKS * LANES
    tm = TOK_TILE
    return pl.pallas_call(
        _ln_kernel,
        grid=(t // tm,),
        in_specs=[pl.BlockSpec((tm, d), lambda i: (i, 0)), _const_spec((1, d)), _const_spec((1, d))],
        out_specs=_tok_spec(tm),
        out_shape=jax.ShapeDtypeStruct((t * ROW_CHUNKS, LANES), jnp.float32),
        compiler_params=_params(("parallel",)),
        name="ln_in",
    )(x, g, b)


def _unpack_kernel(x_ref, o_ref):
    o_ref[...] = _load_rows(x_ref, o_ref.shape[0])


def _unpack_call(x, t):
    tm = TOK_TILE
    d = ROW_CHUNKS * LANES
    return pl.pallas_call(
        _unpack_kernel,
        grid=(t // tm,),
        in_specs=[_tok_spec(tm)],
        out_specs=pl.BlockSpec((tm, d), lambda i: (i, 0)),
        out_shape=jax.ShapeDtypeStruct((t, d), jnp.float32),
        compiler_params=_params(("parallel",)),
        name="unpack",
    )(x)


def _proj_kernel(x_ref, w_ref, b_ref, wvt_ref, bvt_ref, q_ref, k_ref, vt_ref, p_ref, *, attn_dim, scale):
    xb = _load_rows(x_ref, q_ref.shape[0]).astype(jnp.bfloat16)
    acc = jnp.dot(xb, w_ref[...], preferred_element_type=jnp.float32) + b_ref[...]
    q_ref[...] = (acc[:, :attn_dim] * scale).astype(jnp.bfloat16)
    k_ref[...] = acc[:, attn_dim:2 * attn_dim].astype(jnp.bfloat16)
    p_ref[...] = acc[:, 2 * attn_dim:]
    vt = lax.dot_general(wvt_ref[...], xb, (((1,), (1,)), ((), ())), preferred_element_type=jnp.float32)
    vt_ref[...] = (vt + bvt_ref[...]).astype(jnp.bfloat16)


def _proj_call(x, t, lw, attn_dim, pool_dim):
    w, b, wvt, bvt = lw["w_qkp"], lw["b_qkp"], lw["w_v_t"], lw["b_v_t"]
    d, n = w.shape
    tm = TOK_TILE
    row = lambda i: (i, 0)
    return pl.pallas_call(
        functools.partial(_proj_kernel, attn_dim=attn_dim, scale=HEAD_DIM ** -0.5 * LOG2_E),
        grid=(t // tm,),
        in_specs=[_tok_spec(tm), _const_spec((d, n)), _const_spec((1, n)), _const_spec(wvt.shape),
                  _const_spec(bvt.shape)],
        out_specs=[pl.BlockSpec((tm, attn_dim), row), pl.BlockSpec((tm, attn_dim), row),
                   pl.BlockSpec((attn_dim, tm), lambda i: (0, i)), pl.BlockSpec((tm, pool_dim), row)],
        out_shape=[jax.ShapeDtypeStruct((t, attn_dim), jnp.bfloat16),
                   jax.ShapeDtypeStruct((t, attn_dim), jnp.bfloat16),
                   jax.ShapeDtypeStruct((attn_dim, t), jnp.bfloat16),
                   jax.ShapeDtypeStruct((t, pool_dim), jnp.float32)],
        compiler_params=_params(("parallel",)),
        name="proj",
    )(x, w, b, wvt, bvt)


def _attn_kernel(q_ref, k0, k1, k2, vt0, vt1, vt2, tab_ref, o_ref):
    q = q_ref[...]
    k = jnp.concatenate([k0[...], k1[...], k2[...]], axis=0)
    vt = jnp.concatenate([vt0[...], vt1[...], vt2[...]], axis=1)
    lane_head = lax.broadcasted_iota(jnp.int32, (1, LANES), 1) // HEAD_DIM
    heads_per_vreg = LANES // HEAD_DIM

    def scores(h):
        group = slice(h // heads_per_vreg * LANES, (h // heads_per_vreg + 1) * LANES)
        q_h = jnp.where(lane_head == h % heads_per_vreg, q[:, group], jnp.zeros((), q.dtype))
        s = lax.dot_general(k[:, group], q_h, (((1,), (1,)), ((), ())),
                            preferred_element_type=jnp.float32)
        return s + tab_ref[0, h]

    s_next = scores(0)
    for h in range(N_HEADS):
        s = s_next
        if h + 1 < N_HEADS:
            s_next = scores(h + 1)
        m = jnp.max(s, axis=0, keepdims=True)
        e = jnp.exp2(s - m)
        denom = jnp.sum(e, axis=0, keepdims=True)
        rows_h = slice(h * HEAD_DIM, (h + 1) * HEAD_DIM)
        o = jnp.dot(vt[rows_h, :], e.astype(jnp.bfloat16), preferred_element_type=jnp.float32)
        o_ref[rows_h, :] = (o / denom).astype(o_ref.dtype)


def _attn_call(q, k, vt, table, batch, seq):
    t, attn_dim = q.shape
    rows = seq // GRID_W
    assert rows % Q_ROWS == 0 and rows >= WIN_ROWS
    assert Q_ROWS == KV_BLK_ROWS == WIN_ROWS // 2 and KV_ROWS == 3 * KV_BLK_ROWS
    nq = Q_ROWS * GRID_W
    q_blocks = seq // nq
    kv_block = lambda i, b, j: b * q_blocks + jnp.clip(i - 1 + j, 0, q_blocks - 1)

    def table_index(i, b):
        return (jnp.where(i == 0, 0, jnp.where(i == q_blocks - 1, 2, 1)), 0, 0, 0)

    n_kv = KV_ROWS // KV_BLK_ROWS
    k_specs = [pl.BlockSpec((nq, attn_dim), functools.partial(lambda i, b, j: (kv_block(i, b, j), 0), j=j))
               for j in range(n_kv)]
    vt_specs = [pl.BlockSpec((attn_dim, nq), functools.partial(lambda i, b, j: (0, kv_block(i, b, j)), j=j))
                for j in range(n_kv)]
    return pl.pallas_call(
        _attn_kernel,
        grid=(q_blocks, batch),
        in_specs=[pl.BlockSpec((nq, attn_dim), lambda i, b: (b * q_blocks + i, 0))] + k_specs + vt_specs
        + [pl.BlockSpec((1,) + table.shape[1:], table_index)],
        out_specs=pl.BlockSpec((attn_dim, nq), lambda i, b: (0, b * q_blocks + i)),
        out_shape=jax.ShapeDtypeStruct((attn_dim, t), jnp.bfloat16),
        compiler_params=_params(("parallel", "parallel")),
        name="attn",
    )(q, k, k, k, vt, vt, vt, table)


def _bias_table(rpb):
    n_dr = 2 * WIN_ROWS - 1
    n_dc = 2 * WIN_COLS - 1
    n_heads = rpb.shape[0]
    c = jnp.arange(GRID_W)
    col_start = jnp.clip(c - WIN_COLS // 2, 0, GRID_W - WIN_COLS)
    col_ok = (c[None, :] >= col_start[:, None]) & (c[None, :] < col_start[:, None] + WIN_COLS)
    dc = c[None, :] - c[:, None] + (WIN_COLS - 1)
    onehot = ((dc[None] == jnp.arange(n_dc)[:, None, None]) & col_ok[None]).astype(jnp.float32)
    bank = jnp.einsum("hrd,dck->hrck", rpb.astype(jnp.float32) * LOG2_E, onehot,
                      precision=lax.Precision.HIGHEST)
    lead = (KV_ROWS - Q_ROWS) // 2
    i = jnp.arange(Q_ROWS)[:, None]
    j = jnp.arange(KV_ROWS)[None, :]
    row_ok = jnp.stack([(j >= lead) & (j < lead + WIN_ROWS) & (i >= 0),
                        (j - i >= 0) & (j - i < WIN_ROWS),
                        (j < WIN_ROWS) & (i >= 0)])
    first = [WIN_ROWS - 1 - lead - qi for qi in range(Q_ROWS)]
    assert min(first) >= 0 and max(first) + KV_ROWS <= n_dr
    tiles = jnp.stack([bank[:, f:f + KV_ROWS] for f in first], axis=1)
    ok = row_ok[:, None, :, :, None, None] & col_ok[None, None, None, None]
    tab = jnp.where(ok, tiles[None], NEG)
    tab = jnp.transpose(tab, (0, 1, 3, 5, 2, 4))
    return tab.reshape(3, n_heads, KV_ROWS * GRID_W, Q_ROWS * GRID_W)


def _route_class(logits_t):
    m = jnp.max(logits_t, axis=0, keepdims=True)
    e = jnp.exp(logits_t - m)
    probs = e / jnp.sum(e, axis=0, keepdims=True)
    p = [probs[r:r + 1, :] for r in range(N_EXPERTS)]

    def top2_sum(a, b, c, d):
        hi1, lo1 = jnp.maximum(a, b), jnp.minimum(a, b)
        hi2, lo2 = jnp.maximum(c, d), jnp.minimum(c, d)
        return jnp.maximum(hi1, hi2) + jnp.maximum(jnp.minimum(hi1, hi2), jnp.maximum(lo1, lo2))

    score = [top2_sum(*p[4 * g:4 * g + 4]) for g in range(N_GROUPS)]
    best_g = jnp.zeros_like(score[0], dtype=jnp.int32)
    best_s = score[0]
    for g in range(1, N_GROUPS):
        upd = score[g] > best_s
        best_g = jnp.where(upd, g, best_g)
        best_s = jnp.where(upd, score[g], best_s)

    vals = []
    for r in range(EXPERTS_PER_GROUP):
        val = p[r]
        for g in range(1, N_GROUPS):
            val = jnp.where(best_g == g, p[4 * g + r], val)
        vals.append(val)

    def argmax4(vs):
        idx = jnp.zeros_like(best_g)
        top = vs[0]
        for r in range(1, EXPERTS_PER_GROUP):
            upd = vs[r] > top
            idx = jnp.where(upd, r, idx)
            top = jnp.where(upd, vs[r], top)
        return idx

    i1 = argmax4(vals)
    i2 = argmax4([jnp.where(i1 == r, -1.0, vals[r]) for r in range(EXPERTS_PER_GROUP)])
    lo = jnp.minimum(i1, i2)
    hi = jnp.maximum(i1, i2)
    pair = jnp.zeros_like(lo)
    for n, (a, b) in enumerate(PAIRS):
        pair = jnp.where((lo == a) & (hi == b), n, pair)
    return best_g * len(PAIRS) + pair


def _mix_kernel(x_ref, a_ref, p_ref, pprev_ref, pnext_ref, wg_ref, bg_ref, wpool_ref, bpool_ref,
                pscale_ref, woa_ref, wop_ref, wout_ref, bout_ref, lng_ref, lnb_ref, wr_ref, br_ref,
                x1_ref, cls_ref, halo_ref, *, seq, alpha):
    tm = a_ref.shape[1]
    ts = tm // MIX_SUBTILES
    d = wout_ref.shape[1]
    pool_dim = p_ref.shape[1]
    group_dim = pool_dim // len(POOL_WINDOWS)
    tiles_per_seq = seq // tm
    tile_in_seq = pl.program_id(0) % tiles_per_seq
    at_start = tile_in_seq == 0
    at_end = tile_in_seq == tiles_per_seq - 1

    halo_ref[0:POOL_HALO, :] = jnp.where(at_start, 0.0, pprev_ref[...])
    halo_ref[POOL_HALO:POOL_HALO + tm, :] = p_ref[...]
    halo_ref[POOL_HALO + tm:, :] = jnp.where(at_end, 0.0, pnext_ref[...])

    def projections(r0):
        x = _load_rows(x_ref, ts, r0)
        half_z = jnp.dot(x.astype(jnp.bfloat16), wg_ref[...], preferred_element_type=jnp.float32) + bg_ref[...]
        attn_out = lax.dot_general(a_ref[:, r0:r0 + ts], woa_ref[...], (((0,), (0,)), ((), ())),
                                   preferred_element_type=jnp.float32)
        return x, half_z, attn_out

    def window_diffs(r0):
        pos = tile_in_seq * tm + r0 + lax.broadcasted_iota(jnp.int32, (ts, 1), 0)
        base = POOL_HALO + r0
        diffs = []
        for g, w in enumerate(POOL_WINDOWS):
            lanes = slice(g * group_dim, (g + 1) * group_dim)
            total = halo_ref[base - w // 2:base - w // 2 + ts, lanes]
            for off in range(-w // 2 + 1, w // 2):
                total = total + halo_ref[base + off:base + off + ts, lanes]
            count = jnp.clip(pos + w // 2, 0, seq) - jnp.clip(pos - w // 2, 0, seq)
            diff = total / count.astype(jnp.float32) - halo_ref[base:base + ts, lanes]
            diffs.append(diff.astype(jnp.bfloat16))
        return diffs

    def mixer(diffs, half_z, attn_out):
        gates2 = jnp.tanh(half_z) + 1.0
        pooled = []
        for g, diff in enumerate(diffs):
            lanes = slice(g * group_dim, (g + 1) * group_dim)
            y = jnp.dot(diff, wpool_ref[g], preferred_element_type=jnp.float32)
            pooled.append((y + bpool_ref[:, lanes]) * pscale_ref[:, lanes])
        pm = jnp.concatenate(pooled, axis=1)
        pool_out = jnp.dot(pm.astype(jnp.bfloat16), wop_ref[...], preferred_element_type=jnp.float32)
        mix2 = gates2[:, :d] * attn_out + gates2[:, d:] * pool_out
        return jnp.dot(mix2.astype(jnp.bfloat16), wout_ref[...], preferred_element_type=jnp.float32) + bout_ref[...]

    def finish(r0, x, y):
        x1 = _layer_norm(alpha * x + y, lng_ref[...], lnb_ref[...])
        _store_rows(x1_ref, x1, r0)
        x_hi = x1.astype(jnp.bfloat16)
        x_lo = (x1 - x_hi.astype(jnp.float32)).astype(jnp.bfloat16)
        nt = (((1,), (1,)), ((), ()))
        both = lax.dot_general(wr_ref[...], x_hi, nt, preferred_element_type=jnp.float32)
        cross = lax.dot_general(wr_ref[0:N_EXPERTS, :], x_lo, nt, preferred_element_type=jnp.float32)
        logits_t = both[0:N_EXPERTS, :] + both[N_EXPERTS:, :] + cross + br_ref[...]
        cls_ref[0, :, r0:r0 + ts] = _route_class(logits_t)

    starts = [n * ts for n in range(MIX_SUBTILES)]
    diffs = [window_diffs(r0) for r0 in starts]
    staged = [projections(r0) for r0 in starts]
    mixed = [(x, mixer(df, half_z, attn_out)) for df, (x, half_z, attn_out) in zip(diffs, staged)]
    for r0, (x, y) in zip(starts, mixed):
        finish(r0, x, y)


def _mix_call(x, a, p, lw, seq, alpha):
    attn_dim, t = a.shape
    tm = TOK_TILE
    assert seq % tm == 0 and tm % POOL_HALO == 0
    pool_dim = p.shape[1]
    row = lambda i: (i, 0)
    halo_blocks = t // POOL_HALO
    per_tile = tm // POOL_HALO
    prev = lambda i: (jnp.maximum(i * per_tile - 1, 0), 0)
    nxt = lambda i: (jnp.minimum((i + 1) * per_tile, halo_blocks - 1), 0)
    weights = [lw["w_gate"], lw["b_gate"], lw["w_pool"], lw["b_pool"], lw["pool_scale"], lw["w_oa"],
               lw["w_op"], lw["w_out"], lw["b_out"], lw["ln1_g"], lw["ln1_b"], lw["w_router_t"],
               lw["b_router_t"]]
    return pl.pallas_call(
        functools.partial(_mix_kernel, seq=seq, alpha=alpha),
        grid=(t // tm,),
        in_specs=[_tok_spec(tm), pl.BlockSpec((attn_dim, tm), lambda i: (0, i)),
                  pl.BlockSpec((tm, pool_dim), row), pl.BlockSpec((POOL_HALO, pool_dim), prev),
                  pl.BlockSpec((POOL_HALO, pool_dim), nxt)] + [_const_spec(w.shape) for w in weights],
        out_specs=[_tok_spec(tm), pl.BlockSpec((1, 1, tm), lambda i: (i, 0, 0))],
        out_shape=[jax.ShapeDtypeStruct((t * ROW_CHUNKS, LANES), jnp.float32),
                   jax.ShapeDtypeStruct((t // tm, 1, tm), jnp.int32)],
        scratch_shapes=[pltpu.VMEM((tm + 2 * POOL_HALO, pool_dim), jnp.float32)],
        compiler_params=_params(("parallel",)),
        name="mix",
    )(x, a, p, p, p, *weights)


def _rank_kernel(cls_ref, rank_ref, count_ref, carry_ref):
    tm = cls_ref.shape[2]

    @pl.when(pl.program_id(0) == 0)
    def _():
        carry_ref[...] = jnp.zeros_like(carry_ref)

    cls = cls_ref[0]
    onehot = lax.broadcasted_iota(jnp.int32, (CLASS_ROWS, tm), 0) == cls
    upper = (lax.broadcasted_iota(jnp.int32, (tm, tm), 0)
             <= lax.broadcasted_iota(jnp.int32, (tm, tm), 1)).astype(jnp.bfloat16)
    incl = jnp.dot(onehot.astype(jnp.bfloat16), upper, preferred_element_type=jnp.float32)
    before = incl - 1.0 + carry_ref[...]
    rank_ref[0] = jnp.sum(jnp.where(onehot, before, 0.0), axis=0, keepdims=True).astype(jnp.int32)
    carry_ref[...] = carry_ref[...] + incl[:, tm - 1:tm]
    count_ref[...] = carry_ref[...].astype(jnp.int32)


def _rank_call(cls):
    nt, _, tm = cls.shape
    return pl.pallas_call(
        _rank_kernel,
        grid=(nt,),
        in_specs=[pl.BlockSpec((1, 1, tm), lambda i: (i, 0, 0))],
        out_specs=[pl.BlockSpec((1, 1, tm), lambda i: (i, 0, 0)), _const_spec((CLASS_ROWS, 1))],
        out_shape=[jax.ShapeDtypeStruct((nt, 1, tm), jnp.int32),
                   jax.ShapeDtypeStruct((CLASS_ROWS, 1), jnp.int32)],
        scratch_shapes=[pltpu.VMEM((CLASS_ROWS, 1), jnp.float32)],
        compiler_params=_params(("arbitrary",)),
        name="rank",
    )(cls)


def _slot_table_kernel(pad_lo_ref, pad_hi_ref, dest_ref, table_ref):
    i = pl.program_id(0)
    chunk = dest_ref.shape[2]

    @pl.when(i == 0)
    def _():
        def zero(s, c):
            table_ref[s] = 0
            return c
        for cls in range(N_CLASSES + 1):
            lax.fori_loop(pad_lo_ref[cls], pad_hi_ref[cls], zero, 0)

    def place(r, c):
        table_ref[dest_ref[0, 0, r]] = i * chunk + r
        return c
    lax.fori_loop(0, chunk, place, 0, unroll=8)


def _slot_table_call(dest, pad_lo, pad_hi, n_slots):
    n_chunks, _, chunk = dest.shape
    grid_spec = pltpu.PrefetchScalarGridSpec(
        num_scalar_prefetch=2,
        grid=(n_chunks,),
        in_specs=[pl.BlockSpec((1, 1, chunk), lambda i, lo, hi: (i, 0, 0), memory_space=pltpu.SMEM)],
        out_specs=pl.BlockSpec(memory_space=pltpu.SMEM),
    )
    return pl.pallas_call(
        _slot_table_kernel,
        grid_spec=grid_spec,
        out_shape=jax.ShapeDtypeStruct((n_slots,), jnp.int32),
        compiler_params=_params(("arbitrary",)),
        name="slot_table",
    )(pad_lo, pad_hi, dest)


def _ffn_kernel(ea_ref, eb_ref, nvalid_ref, tok_ref, tok_next_ref, tok_prev_ref, x_hbm,
                w1a_ref, b1a_ref, w2a_ref, b2a_ref, w1b_ref, b1b_ref, w2b_ref, b2b_ref,
                wr_ref, br_ref, lng_ref, lnb_ref, out_hbm, xbuf, obuf, sem_in, sem_out,
                *, alpha, n_tok):
    blk = pl.program_id(0)
    last = pl.num_programs(0) - 1
    slot = blk % 2
    other = 1 - slot
    n_valid = nvalid_ref[blk]
    n_valid_prev = jnp.where(blk > 0, nvalid_ref[jnp.maximum(blk - 1, 0)], 0)
    rows = MOE_BLOCK
    tile = ROW_CHUNKS

    def gather_row(ids_ref, dst_slot, r):
        src = pl.multiple_of(ids_ref[0, 0, r] * tile, tile)
        return pltpu.make_async_copy(x_hbm.at[pl.ds(src, tile)], xbuf.at[dst_slot, pl.ds(r * tile, tile)],
                                     sem_in.at[dst_slot])

    def scatter_row(ids_ref, count, src_slot, r):
        token = jnp.where(r < count, ids_ref[0, 0, r], n_tok + r)
        dst = pl.multiple_of(token * tile, tile)
        return pltpu.make_async_copy(obuf.at[src_slot, pl.ds(r * tile, tile)], out_hbm.at[pl.ds(dst, tile)],
                                     sem_out.at[src_slot])

    def wait_gather(dst_slot):
        pltpu.make_async_copy(x_hbm.at[pl.ds(0, rows * tile)], xbuf.at[dst_slot], sem_in.at[dst_slot]).wait()

    def wait_scatter(src_slot):
        pltpu.make_async_copy(obuf.at[src_slot], out_hbm.at[pl.ds(0, rows * tile)], sem_out.at[src_slot]).wait()

    def start_all(make_copy):
        lax.fori_loop(0, rows, lambda r, c: (make_copy(r).start(), c)[1], 0, unroll=8)

    @pl.when(blk == 0)
    def _():
        obuf[1] = jnp.zeros(obuf.shape[1:], obuf.dtype)
        start_all(functools.partial(gather_row, tok_ref, 0))

    wait_gather(slot)

    def start_neighbours(r):
        gather_row(tok_next_ref, other, r).start()
        return scatter_row(tok_prev_ref, n_valid_prev, other, r)

    @pl.when(n_valid == 0)
    def _():
        start_all(start_neighbours)

    @pl.when(n_valid > 0)
    def _():
        for r in range(rows):
            start_neighbours(r).start()
        x = _load_rows(xbuf.at[slot], rows)
        xb = x.astype(jnp.bfloat16)

        def up(w1_ref, b1_ref):
            return jnp.dot(xb, w1_ref[0], preferred_element_type=jnp.float32) + b1_ref[0]

        def down(h, w2_ref, b2_ref):
            h = 0.5 * h * (1.0 + lax.erf(h * (2.0 ** -0.5)))
            return jnp.dot(h.astype(jnp.bfloat16), w2_ref[0], preferred_element_type=jnp.float32) + b2_ref[0]

        h_a = up(w1a_ref, b1a_ref)
        h_b = up(w1b_ref, b1b_ref)

        ea = ea_ref[blk]
        eb = eb_ref[blk]
        w_diff = wr_ref[pl.ds(eb, 1), :] - wr_ref[pl.ds(ea, 1), :]
        delta = jnp.sum(x * w_diff, axis=-1, keepdims=True) + (br_ref[eb] - br_ref[ea])
        gate_a = 1.0 / (1.0 + jnp.exp(delta))
        gate_b = 1.0 / (1.0 + jnp.exp(-delta))
        y = gate_a * down(h_a, w2a_ref, b2a_ref) + gate_b * down(h_b, w2b_ref, b2b_ref)
        _store_rows(obuf.at[slot], _layer_norm(alpha * x + y, lng_ref[...], lnb_ref[...]))

    wait_scatter(other)

    @pl.when(blk == last)
    def _():
        wait_gather(other)
        start_all(functools.partial(scatter_row, tok_ref, n_valid, slot))
        wait_scatter(slot)


def _ffn_call(x1, t, buf_tok, block_ea, block_eb, block_nvalid, lw, alpha):
    n_blocks = block_ea.shape[0]
    _, d, d_ff = lw["w1"].shape
    const = lambda shape: pl.BlockSpec(shape, lambda b, ea, eb, nv: (0,) * len(shape))
    by_a = lambda shape: pl.BlockSpec(shape, lambda b, ea, eb, nv: (ea[b], 0, 0))
    by_b = lambda shape: pl.BlockSpec(shape, lambda b, ea, eb, nv: (eb[b], 0, 0))
    ids = lambda shift: pl.BlockSpec(
        (1, 1, MOE_BLOCK), lambda b, ea, eb, nv: (jnp.clip(b + shift, 0, n_blocks - 1), 0, 0),
        memory_space=pltpu.SMEM)
    grid_spec = pltpu.PrefetchScalarGridSpec(
        num_scalar_prefetch=3,
        grid=(n_blocks,),
        in_specs=[
            ids(0), ids(1), ids(-1),
            pl.BlockSpec(memory_space=pl.ANY),
            by_a((1, d, d_ff)), by_a((1, 1, d_ff)), by_a((1, d_ff, d)), by_a((1, 1, d)),
            by_b((1, d, d_ff)), by_b((1, 1, d_ff)), by_b((1, d_ff, d)), by_b((1, 1, d)),
            const((N_EXPERTS, d)),
            pl.BlockSpec(memory_space=pltpu.SMEM),
            const((1, d)), const((1, d)),
        ],
        out_specs=pl.BlockSpec(memory_space=pl.ANY),
        scratch_shapes=[pltpu.VMEM((2, MOE_BLOCK * ROW_CHUNKS, LANES), jnp.float32),
                        pltpu.VMEM((2, MOE_BLOCK * ROW_CHUNKS, LANES), jnp.float32),
                        pltpu.SemaphoreType.DMA((2,)), pltpu.SemaphoreType.DMA((2,))],
    )
    return pl.pallas_call(
        functools.partial(_ffn_kernel, alpha=alpha, n_tok=t),
        grid_spec=grid_spec,
        out_shape=jax.ShapeDtypeStruct(((t + MOE_BLOCK) * ROW_CHUNKS, LANES), jnp.float32),
        compiler_params=_params(("arbitrary",)),
        name="ffn",
    )(block_ea, block_eb, block_nvalid, buf_tok, buf_tok, buf_tok, x1,
      lw["w1"], lw["b1"], lw["w2"], lw["b2"], lw["w1"], lw["b1"], lw["w2"], lw["b2"],
      lw["w_router_rows"], lw["b_router"], lw["ln2_g"], lw["ln2_b"])


def _dispatch_tables(cls, rank, counts, t):
    counts = counts[:N_CLASSES, 0]
    n_blocks = -(-(t + N_CLASSES * (MOE_BLOCK - 1)) // MOE_BLOCK)
    padded = (counts + MOE_BLOCK - 1) // MOE_BLOCK * MOE_BLOCK
    pad_ends = jnp.cumsum(padded)
    pad_starts = pad_ends - padded
    dest = pad_starts[cls] + rank
    n_slots = n_blocks * MOE_BLOCK
    pad_lo = jnp.concatenate([pad_starts + counts, pad_ends[-1:]]).astype(jnp.int32)
    pad_hi = jnp.concatenate([pad_ends, jnp.full((1,), n_slots, pad_ends.dtype)]).astype(jnp.int32)
    buf_tok = _slot_table_call(dest, pad_lo, pad_hi, n_slots)
    block_start = jnp.arange(n_blocks, dtype=jnp.int32) * MOE_BLOCK
    block_cls = jnp.minimum(jnp.sum(pad_ends[None, :] <= block_start[:, None], axis=1), N_CLASSES - 1)
    block_cls = block_cls.astype(jnp.int32)
    n_valid = jnp.clip(counts[block_cls] - (block_start - pad_starts[block_cls]), 0, MOE_BLOCK)
    pair = jnp.asarray(PAIRS, jnp.int32)[block_cls % len(PAIRS)]
    group = block_cls // len(PAIRS)
    ea = group * EXPERTS_PER_GROUP + pair[:, 0]
    eb = group * EXPERTS_PER_GROUP + pair[:, 1]
    return (buf_tok.reshape(n_blocks, 1, MOE_BLOCK), ea.astype(jnp.int32), eb.astype(jnp.int32),
            n_valid.astype(jnp.int32))


def kernel(x_prompt, x_sample, ln_in_g, ln_in_b, w_in, b_in, rpb, w_pool, b_pool, pool_scale, w_oa, w_op,
           w_out, b_out, ln1_g, ln1_b, w_router, b_router, w1, b1, w2, b2, ln2_g, ln2_b):
    depth, d, _ = w_in.shape
    attn_dim = N_HEADS * HEAD_DIM
    pool_dim = w_op.shape[1]
    qkvp = 3 * attn_dim + pool_dim
    alpha = (2 * depth) ** 0.25
    bf = jnp.bfloat16
    f32 = jnp.float32

    wr_hi = w_router.T.astype(bf)
    wr_lo = (w_router.T - wr_hi.astype(f32)).astype(bf)
    shared = {
        "w_router_t": jnp.concatenate([wr_hi, wr_lo], axis=0),
        "b_router_t": b_router.reshape(N_EXPERTS, 1).astype(f32),
        "w_router_rows": w_router.T.astype(f32),
        "b_router": b_router.astype(f32),
    }
    layers = []
    for l in range(depth):
        lw = dict(shared)
        lw.update({
            "w_qkp": jnp.concatenate([w_in[l, :, :2 * attn_dim], w_in[l, :, 3 * attn_dim:qkvp]],
                                     axis=1).astype(bf),
            "b_qkp": jnp.concatenate([b_in[l, :2 * attn_dim], b_in[l, 3 * attn_dim:qkvp]]).reshape(1, -1),
            "w_v_t": w_in[l, :, 2 * attn_dim:3 * attn_dim].T.astype(bf),
            "b_v_t": b_in[l, 2 * attn_dim:3 * attn_dim].reshape(-1, 1),
            "w_gate": (0.5 * w_in[l, :, qkvp:]).astype(bf), "b_gate": 0.5 * b_in[l, qkvp:].reshape(1, -1),
            "table": _bias_table(rpb[l]),
            "w_pool": w_pool[l].astype(bf), "b_pool": b_pool[l].reshape(1, -1),
            "pool_scale": pool_scale[l].reshape(1, -1),
            "w_oa": w_oa[l].astype(bf), "w_op": w_op[l].astype(bf), "w_out": (0.5 * w_out[l]).astype(bf),
            "b_out": b_out[l].reshape(1, -1),
            "ln1_g": ln1_g[l].reshape(1, -1), "ln1_b": ln1_b[l].reshape(1, -1),
            "w1": w1[l].astype(bf), "b1": b1[l].reshape(N_EXPERTS, 1, -1),
            "w2": w2[l].astype(bf), "b2": b2[l].reshape(N_EXPERTS, 1, -1),
            "ln2_g": ln2_g[l].reshape(1, -1), "ln2_b": ln2_b[l].reshape(1, -1),
        })
        layers.append(lw)

    def run(x3):
        batch, seq, _ = x3.shape
        t = batch * seq
        x = _ln_call(x3.reshape(t, d), ln_in_g.reshape(1, -1), ln_in_b.reshape(1, -1))
        for lw in layers:
            q, k, vt, p = _proj_call(x, t, lw, attn_dim, pool_dim)
            a = _attn_call(q, k, vt, lw["table"], batch, seq)
            x1, cls = _mix_call(x, a, p, lw, seq, alpha)
            rank, counts = _rank_call(cls)
            buf_tok, ea, eb, n_valid = _dispatch_tables(cls, rank, counts, t)
            x = _ffn_call(x1, t, buf_tok, ea, eb, n_valid, lw, alpha)
        return _unpack_call(x, t).reshape(batch, seq, d)

    return (run(x_prompt), run(x_sample))
```

```python
import functools

import jax
import jax.numpy as jnp
from jax import lax
from jax.experimental import pallas as pl
from jax.experimental.pallas import tpu as pltpu

GRID_W = 64
N_HEADS = 8
HEAD_DIM = 64
WIN_ROWS = 8
WIN_COLS = 16
POOL_WINDOWS = (2, 4, 8, 16)
N_EXPERTS = 16
N_GROUPS = 4
EXPERTS_PER_GROUP = 4
PAIRS = ((0, 1), (0, 2), (0, 3), (1, 2), (1, 3), (2, 3))
N_CLASSES = N_GROUPS * len(PAIRS)
CLASS_ROWS = 32
MOE_BLOCK = 256
LN_EPS = 1e-5
NEG = -1e30

LANES = 128
ROW_CHUNKS = 8
LOG2_E = 1.4426950408889634
Q_ROWS = 4
KV_ROWS = 12
KV_BLK_ROWS = 4
TOK_TILE = 512
MIX_SUBTILES = 1
POOL_HALO = 8
VMEM_LIMIT = 56 * 1024 * 1024


def _layer_norm(x, g, b):
    mu = jnp.mean(x, axis=-1, keepdims=True)
    xc = x - mu
    var = jnp.mean(xc * xc, axis=-1, keepdims=True)
    return xc * lax.rsqrt(var + LN_EPS) * g + b


def _params(semantics):
    return pltpu.CompilerParams(dimension_semantics=semantics, vmem_limit_bytes=VMEM_LIMIT)


def _const_spec(shape):
    nd = len(shape)
    return pl.BlockSpec(shape, lambda *_: (0,) * nd)


def _load_rows(ref, n_tok, tok0=0):
    return jnp.concatenate([ref[pl.ds(tok0 * ROW_CHUNKS + s, n_tok, stride=ROW_CHUNKS), :]
                            for s in range(ROW_CHUNKS)], axis=1)


def _store_rows(ref, value, tok0=0):
    n_tok = value.shape[0]
    for s in range(ROW_CHUNKS):
        ref[pl.ds(tok0 * ROW_CHUNKS + s, n_tok, stride=ROW_CHUNKS), :] = value[:, s * LANES:(s + 1) * LANES]


def _tok_spec(tm):
    return pl.BlockSpec((tm * ROW_CHUNKS, LANES), lambda i: (i, 0))


def _ln_kernel(x_ref, g_ref, b_ref, o_ref):
    _store_rows(o_ref, _layer_norm(x_ref[...], g_ref[...], b_ref[...]))


def _ln_call(x, g, b):
    t, d = x.shape
    assert d == ROW_CHUNKS * LANES
    tm = TOK_TILE
    return pl.pallas_call(
        _ln_kernel,
        grid=(t // tm,),
        in_specs=[pl.BlockSpec((tm, d), lambda i: (i, 0)), _const_spec((1, d)), _const_spec((1, d))],
        out_specs=_tok_spec(tm),
        out_shape=jax.ShapeDtypeStruct((t * ROW_CHUNKS, LANES), jnp.float32),
        compiler_params=_params(("parallel",)),
        name="ln_in",
    )(x, g, b)


def _unpack_kernel(x_ref, o_ref):
    o_ref[...] = _load_rows(x_ref, o_ref.shape[0])


def _unpack_call(x, t):
    tm = TOK_TILE
    d = ROW_CHUNKS * LANES
    return pl.pallas_call(
        _unpack_kernel,
        grid=(t // tm,),
        in_specs=[_tok_spec(tm)],
        out_specs=pl.BlockSpec((tm, d), lambda i: (i, 0)),
        out_shape=jax.ShapeDtypeStruct((t, d), jnp.float32),
        compiler_params=_params(("parallel",)),
        name="unpack",
    )(x)


def _proj_kernel(x_ref, w_ref, b_ref, wvt_ref, bvt_ref, q_ref, k_ref, vt_ref, p_ref, *, attn_dim, scale):
    xb = _load_rows(x_ref, q_ref.shape[0]).astype(jnp.bfloat16)
    acc = jnp.dot(xb, w_ref[...], preferred_element_type=jnp.float32) + b_ref[...]
    q_ref[...] = (acc[:, :attn_dim] * scale).astype(jnp.bfloat16)
    k_ref[...] = acc[:, attn_dim:2 * attn_dim].astype(jnp.bfloat16)
    p_ref[...] = acc[:, 2 * attn_dim:]
    vt = lax.dot_general(wvt_ref[...], xb, (((1,), (1,)), ((), ())), preferred_element_type=jnp.float32)
    vt_ref[...] = (vt + bvt_ref[...]).astype(jnp.bfloat16)


def _proj_call(x, t, lw, attn_dim, pool_dim):
    w, b, wvt, bvt = lw["w_qkp"], lw["b_qkp"], lw["w_v_t"], lw["b_v_t"]
    d, n = w.shape
    tm = TOK_TILE
    row = lambda i: (i, 0)
    return pl.pallas_call(
        functools.partial(_proj_kernel, attn_dim=attn_dim, scale=HEAD_DIM ** -0.5 * LOG2_E),
        grid=(t // tm,),
        in_specs=[_tok_spec(tm), _const_spec((d, n)), _const_spec((1, n)), _const_spec(wvt.shape),
                  _const_spec(bvt.shape)],
        out_specs=[pl.BlockSpec((tm, attn_dim), row), pl.BlockSpec((tm, attn_dim), row),
                   pl.BlockSpec((attn_dim, tm), lambda i: (0, i)), pl.BlockSpec((tm, pool_dim), row)],
        out_shape=[jax.ShapeDtypeStruct((t, attn_dim), jnp.bfloat16),
                   jax.ShapeDtypeStruct((t, attn_dim), jnp.bfloat16),
                   jax.ShapeDtypeStruct((attn_dim, t), jnp.bfloat16),
                   jax.ShapeDtypeStruct((t, pool_dim), jnp.float32)],
        compiler_params=_params(("parallel",)),
        name="proj",
    )(x, w, b, wvt, bvt)


def _attn_kernel(q_ref, k0, k1, k2, vt0, vt1, vt2, tab_ref, o_ref):
    q = q_ref[...]
    k = jnp.concatenate([k0[...], k1[...], k2[...]], axis=0)
    vt = jnp.concatenate([vt0[...], vt1[...], vt2[...]], axis=1)
    lane_head = lax.broadcasted_iota(jnp.int32, (1, LANES), 1) // HEAD_DIM
    heads_per_vreg = LANES // HEAD_DIM

    def scores(h):
        group = slice(h // heads_per_vreg * LANES, (h // heads_per_vreg + 1) * LANES)
        q_h = jnp.where(lane_head == h % heads_per_vreg, q[:, group], jnp.zeros((), q.dtype))
        s = lax.dot_general(k[:, group], q_h, (((1,), (1,)), ((), ())),
                            preferred_element_type=jnp.float32)
        return s + tab_ref[0, h]

    s_next = scores(0)
    for h in range(N_HEADS):
        s = s_next
        if h + 1 < N_HEADS:
            s_next = scores(h + 1)
        m = jnp.max(s, axis=0, keepdims=True)
        e = jnp.exp2(s - m)
        denom = jnp.sum(e, axis=0, keepdims=True)
        rows_h = slice(h * HEAD_DIM, (h + 1) * HEAD_DIM)
        o = jnp.dot(vt[rows_h, :], e.astype(jnp.bfloat16), preferred_element_type=jnp.float32)
        o_ref[rows_h, :] = (o / denom).astype(o_ref.dtype)


def _attn_call(q, k, vt, table, batch, seq):
    t, attn_dim = q.shape
    rows = seq // GRID_W
    assert rows % Q_ROWS == 0 and rows >= WIN_ROWS
    assert Q_ROWS == KV_BLK_ROWS == WIN_ROWS // 2 and KV_ROWS == 3 * KV_BLK_ROWS
    nq = Q_ROWS * GRID_W
    q_blocks = seq // nq
    kv_block = lambda i, b, j: b * q_blocks + jnp.clip(i - 1 + j, 0, q_blocks - 1)

    def table_index(i, b):
        return (jnp.where(i == 0, 0, jnp.where(i == q_blocks - 1, 2, 1)), 0, 0, 0)

    n_kv = KV_ROWS // KV_BLK_ROWS
    k_specs = [pl.BlockSpec((nq, attn_dim), functools.partial(lambda i, b, j: (kv_block(i, b, j), 0), j=j))
               for j in range(n_kv)]
    vt_specs = [pl.BlockSpec((attn_dim, nq), functools.partial(lambda i, b, j: (0, kv_block(i, b, j)), j=j))
                for j in range(n_kv)]
    return pl.pallas_call(
        _attn_kernel,
        grid=(q_blocks, batch),
        in_specs=[pl.BlockSpec((nq, attn_dim), lambda i, b: (b * q_blocks + i, 0))] + k_specs + vt_specs
        + [pl.BlockSpec((1,) + table.shape[1:], table_index)],
        out_specs=pl.BlockSpec((attn_dim, nq), lambda i, b: (0, b * q_blocks + i)),
        out_shape=jax.ShapeDtypeStruct((attn_dim, t), jnp.bfloat16),
        compiler_params=_params(("parallel", "parallel")),
        name="attn",
    )(q, k, k, k, vt, vt, vt, table)


def _bias_table(rpb):
    n_dr = 2 * WIN_ROWS - 1
    n_dc = 2 * WIN_COLS - 1
    n_heads = rpb.shape[0]
    c = jnp.arange(GRID_W)
    col_start = jnp.clip(c - WIN_COLS // 2, 0, GRID_W - WIN_COLS)
    col_ok = (c[None, :] >= col_start[:, None]) & (c[None, :] < col_start[:, None] + WIN_COLS)
    dc = c[None, :] - c[:, None] + (WIN_COLS - 1)
    onehot = ((dc[None] == jnp.arange(n_dc)[:, None, None]) & col_ok[None]).astype(jnp.float32)
    bank = jnp.einsum("hrd,dck->hrck", rpb.astype(jnp.float32) * LOG2_E, onehot,
                      precision=lax.Precision.HIGHEST)
    lead = (KV_ROWS - Q_ROWS) // 2
    i = jnp.arange(Q_ROWS)[:, None]
    j = jnp.arange(KV_ROWS)[None, :]
    row_ok = jnp.stack([(j >= lead) & (j < lead + WIN_ROWS) & (i >= 0),
                        (j - i >= 0) & (j - i < WIN_ROWS),
                        (j < WIN_ROWS) & (i >= 0)])
    first = [WIN_ROWS - 1 - lead - qi for qi in range(Q_ROWS)]
    assert min(first) >= 0 and max(first) + KV_ROWS <= n_dr
    tiles = jnp.stack([bank[:, f:f + KV_ROWS] for f in first], axis=1)
    ok = row_ok[:, None, :, :, None, None] & col_ok[None, None, None, None]
    tab = jnp.where(ok, tiles[None], NEG)
    tab = jnp.transpose(tab, (0, 1, 3, 5, 2, 4))
    return tab.reshape(3, n_heads, KV_ROWS * GRID_W, Q_ROWS * GRID_W)


def _route_class(logits_t):
    m = jnp.max(logits_t, axis=0, keepdims=True)
    e = jnp.exp(logits_t - m)
    probs = e / jnp.sum(e, axis=0, keepdims=True)
    p = [probs[r:r + 1, :] for r in range(N_EXPERTS)]

    def top2_sum(a, b, c, d):
        hi1, lo1 = jnp.maximum(a, b), jnp.minimum(a, b)
        hi2, lo2 = jnp.maximum(c, d), jnp.minimum(c, d)
        return jnp.maximum(hi1, hi2) + jnp.maximum(jnp.minimum(hi1, hi2), jnp.maximum(lo1, lo2))

    score = [top2_sum(*p[4 * g:4 * g + 4]) for g in range(N_GROUPS)]
    best_g = jnp.zeros_like(score[0], dtype=jnp.int32)
    best_s = score[0]
    for g in range(1, N_GROUPS):
        upd = score[g] > best_s
        best_g = jnp.where(upd, g, best_g)
        best_s = jnp.where(upd, score[g], best_s)

    vals = []
    for r in range(EXPERTS_PER_GROUP):
        val = p[r]
        for g in range(1, N_GROUPS):
            val = jnp.where(best_g == g, p[4 * g + r], val)
        vals.append(val)

    def argmax4(vs):
        idx = jnp.zeros_like(best_g)
        top = vs[0]
        for r in range(1, EXPERTS_PER_GROUP):
            upd = vs[r] > top
            idx = jnp.where(upd, r, idx)
            top = jnp.where(upd, vs[r], top)
        return idx

    i1 = argmax4(vals)
    i2 = argmax4([jnp.where(i1 == r, -1.0, vals[r]) for r in range(EXPERTS_PER_GROUP)])
    lo = jnp.minimum(i1, i2)
    hi = jnp.maximum(i1, i2)
    pair = jnp.zeros_like(lo)
    for n, (a, b) in enumerate(PAIRS):
        pair = jnp.where((lo == a) & (hi == b), n, pair)
    return best_g * len(PAIRS) + pair


def _mix_kernel(x_ref, a_ref, p_ref, pprev_ref, pnext_ref, wg_ref, bg_ref, wpool_ref, bpool_ref,
                pscale_ref, woa_ref, wop_ref, wout_ref, bout_ref, lng_ref, lnb_ref, wr_ref, br_ref,
                x1_ref, cls_ref, cnt_ref, halo_ref, *, seq, alpha):
    tm = a_ref.shape[1]
    ts = tm // MIX_SUBTILES
    d = wout_ref.shape[1]
    pool_dim = p_ref.shape[1]
    group_dim = pool_dim // len(POOL_WINDOWS)
    tiles_per_seq = seq // tm
    tile_in_seq = pl.program_id(0) % tiles_per_seq
    at_start = tile_in_seq == 0
    at_end = tile_in_seq == tiles_per_seq - 1

    halo_ref[0:POOL_HALO, :] = jnp.where(at_start, 0.0, pprev_ref[...])
    halo_ref[POOL_HALO:POOL_HALO + tm, :] = p_ref[...]
    halo_ref[POOL_HALO + tm:, :] = jnp.where(at_end, 0.0, pnext_ref[...])

    def projections(r0):
        x = _load_rows(x_ref, ts, r0)
        half_z = jnp.dot(x.astype(jnp.bfloat16), wg_ref[...], preferred_element_type=jnp.float32) + bg_ref[...]
        attn_out = lax.dot_general(a_ref[:, r0:r0 + ts], woa_ref[...], (((0,), (0,)), ((), ())),
                                   preferred_element_type=jnp.float32)
        return x, half_z, attn_out

    def window_diffs(r0):
        pos = tile_in_seq * tm + r0 + lax.broadcasted_iota(jnp.int32, (ts, 1), 0)
        base = POOL_HALO + r0
        diffs = []
        for g, w in enumerate(POOL_WINDOWS):
            lanes = slice(g * group_dim, (g + 1) * group_dim)
            total = halo_ref[base - w // 2:base - w // 2 + ts, lanes]
            for off in range(-w // 2 + 1, w // 2):
                total = total + halo_ref[base + off:base + off + ts, lanes]
            count = jnp.clip(pos + w // 2, 0, seq) - jnp.clip(pos - w // 2, 0, seq)
            diff = total / count.astype(jnp.float32) - halo_ref[base:base + ts, lanes]
            diffs.append(diff.astype(jnp.bfloat16))
        return diffs

    def mixer(diffs, half_z, attn_out):
        gates2 = jnp.tanh(half_z) + 1.0
        pooled = []
        for g, diff in enumerate(diffs):
            lanes = slice(g * group_dim, (g + 1) * group_dim)
            y = jnp.dot(diff, wpool_ref[g], preferred_element_type=jnp.float32)
            pooled.append((y + bpool_ref[:, lanes]) * pscale_ref[:, lanes])
        pm = jnp.concatenate(pooled, axis=1)
        pool_out = jnp.dot(pm.astype(jnp.bfloat16), wop_ref[...], preferred_element_type=jnp.float32)
        mix2 = gates2[:, :d] * attn_out + gates2[:, d:] * pool_out
        return jnp.dot(mix2.astype(jnp.bfloat16), wout_ref[...], preferred_element_type=jnp.float32) + bout_ref[...]

    def finish(r0, x, y):
        x1 = _layer_norm(alpha * x + y, lng_ref[...], lnb_ref[...])
        _store_rows(x1_ref, x1, r0)
        x_hi = x1.astype(jnp.bfloat16)
        x_lo = (x1 - x_hi.astype(jnp.float32)).astype(jnp.bfloat16)
        nt = (((1,), (1,)), ((), ()))
        both = lax.dot_general(wr_ref[...], x_hi, nt, preferred_element_type=jnp.float32)
        cross = lax.dot_general(wr_ref[0:N_EXPERTS, :], x_lo, nt, preferred_element_type=jnp.float32)
        logits_t = both[0:N_EXPERTS, :] + both[N_EXPERTS:, :] + cross + br_ref[...]
        cls_ref[0, :, r0:r0 + ts] = _route_class(logits_t)

    starts = [n * ts for n in range(MIX_SUBTILES)]
    diffs = [window_diffs(r0) for r0 in starts]
    staged = [projections(r0) for r0 in starts]
    mixed = [(x, mixer(df, half_z, attn_out)) for df, (x, half_z, attn_out) in zip(diffs, staged)]
    for r0, (x, y) in zip(starts, mixed):
        finish(r0, x, y)

    onehot = lax.broadcasted_iota(jnp.int32, (CLASS_ROWS, tm), 0) == cls_ref[0]
    cnt_ref[0] = jnp.sum(onehot.astype(jnp.float32), axis=1, keepdims=True).astype(jnp.int32)


def _mix_call(x, a, p, lw, seq, alpha):
    attn_dim, t = a.shape
    tm = TOK_TILE
    assert seq % tm == 0 and tm % POOL_HALO == 0
    pool_dim = p.shape[1]
    row = lambda i: (i, 0)
    halo_blocks = t // POOL_HALO
    per_tile = tm // POOL_HALO
    prev = lambda i: (jnp.maximum(i * per_tile - 1, 0), 0)
    nxt = lambda i: (jnp.minimum((i + 1) * per_tile, halo_blocks - 1), 0)
    weights = [lw["w_gate"], lw["b_gate"], lw["w_pool"], lw["b_pool"], lw["pool_scale"], lw["w_oa"],
               lw["w_op"], lw["w_out"], lw["b_out"], lw["ln1_g"], lw["ln1_b"], lw["w_router_t"],
               lw["b_router_t"]]
    return pl.pallas_call(
        functools.partial(_mix_kernel, seq=seq, alpha=alpha),
        grid=(t // tm,),
        in_specs=[_tok_spec(tm), pl.BlockSpec((attn_dim, tm), lambda i: (0, i)),
                  pl.BlockSpec((tm, pool_dim), row), pl.BlockSpec((POOL_HALO, pool_dim), prev),
                  pl.BlockSpec((POOL_HALO, pool_dim), nxt)] + [_const_spec(w.shape) for w in weights],
        out_specs=[_tok_spec(tm), pl.BlockSpec((1, 1, tm), lambda i: (i, 0, 0)),
                   pl.BlockSpec((1, CLASS_ROWS, 1), lambda i: (i, 0, 0))],
        out_shape=[jax.ShapeDtypeStruct((t * ROW_CHUNKS, LANES), jnp.float32),
                   jax.ShapeDtypeStruct((t // tm, 1, tm), jnp.int32),
                   jax.ShapeDtypeStruct((t // tm, CLASS_ROWS, 1), jnp.int32)],
        scratch_shapes=[pltpu.VMEM((tm + 2 * POOL_HALO, pool_dim), jnp.float32)],
        compiler_params=_params(("parallel",)),
        name="mix",
    )(x, a, p, p, p, *weights)


def _rank_kernel(cls_ref, start_ref, dest_ref, carry_ref):
    tm = cls_ref.shape[2]

    @pl.when(pl.program_id(0) == 0)
    def _():
        carry_ref[...] = start_ref[...]

    cls = cls_ref[0]
    onehot = lax.broadcasted_iota(jnp.int32, (CLASS_ROWS, tm), 0) == cls
    upper = (lax.broadcasted_iota(jnp.int32, (tm, tm), 0)
             <= lax.broadcasted_iota(jnp.int32, (tm, tm), 1)).astype(jnp.bfloat16)
    incl = jnp.dot(onehot.astype(jnp.bfloat16), upper, preferred_element_type=jnp.float32)
    slot = incl - 1.0 + carry_ref[...]
    dest_ref[0] = jnp.sum(jnp.where(onehot, slot, 0.0), axis=0, keepdims=True).astype(jnp.int32)
    carry_ref[...] = carry_ref[...] + incl[:, tm - 1:tm]


def _rank_call(cls, class_start):
    nt, _, tm = cls.shape
    return pl.pallas_call(
        _rank_kernel,
        grid=(nt,),
        in_specs=[pl.BlockSpec((1, 1, tm), lambda i: (i, 0, 0)), _const_spec((CLASS_ROWS, 1))],
        out_specs=pl.BlockSpec((1, 1, tm), lambda i: (i, 0, 0)),
        out_shape=jax.ShapeDtypeStruct((nt, 1, tm), jnp.int32),
        scratch_shapes=[pltpu.VMEM((CLASS_ROWS, 1), jnp.float32)],
        compiler_params=_params(("arbitrary",)),
        name="rank",
    )(cls, class_start)


def _slot_table_kernel(pad_lo_ref, pad_hi_ref, dest_ref, table_ref):
    i = pl.program_id(0)
    chunk = dest_ref.shape[2]

    @pl.when(i == 0)
    def _():
        def zero(s, c):
            table_ref[s] = 0
            return c
        for cls in range(N_CLASSES + 1):
            lax.fori_loop(pad_lo_ref[cls], pad_hi_ref[cls], zero, 0)

    def place(r, c):
        table_ref[dest_ref[0, 0, r]] = i * chunk + r
        return c
    lax.fori_loop(0, chunk, place, 0, unroll=8)


def _slot_table_call(dest, pad_lo, pad_hi, n_slots):
    n_chunks, _, chunk = dest.shape
    grid_spec = pltpu.PrefetchScalarGridSpec(
        num_scalar_prefetch=2,
        grid=(n_chunks,),
        in_specs=[pl.BlockSpec((1, 1, chunk), lambda i, lo, hi: (i, 0, 0), memory_space=pltpu.SMEM)],
        out_specs=pl.BlockSpec(memory_space=pltpu.SMEM),
    )
    return pl.pallas_call(
        _slot_table_kernel,
        grid_spec=grid_spec,
        out_shape=jax.ShapeDtypeStruct((n_slots,), jnp.int32),
        compiler_params=_params(("arbitrary",)),
        name="slot_table",
    )(pad_lo, pad_hi, dest)


def _ffn_kernel(ea_ref, eb_ref, nvalid_ref, tok_ref, tok_next_ref, tok_prev_ref, x_hbm,
                w1a_ref, b1a_ref, w2a_ref, b2a_ref, w1b_ref, b1b_ref, w2b_ref, b2b_ref,
                wr_ref, br_ref, lng_ref, lnb_ref, out_hbm, xbuf, obuf, sem_in, sem_out,
                *, alpha, n_tok):
    blk = pl.program_id(0)
    last = pl.num_programs(0) - 1
    slot = blk % 2
    other = 1 - slot
    n_valid = nvalid_ref[blk]
    n_valid_prev = jnp.where(blk > 0, nvalid_ref[jnp.maximum(blk - 1, 0)], 0)
    rows = MOE_BLOCK
    tile = ROW_CHUNKS

    def gather_row(ids_ref, dst_slot, r):
        src = pl.multiple_of(ids_ref[0, 0, r] * tile, tile)
        return pltpu.make_async_copy(x_hbm.at[pl.ds(src, tile)], xbuf.at[dst_slot, pl.ds(r * tile, tile)],
                                     sem_in.at[dst_slot])

    def scatter_row(ids_ref, count, src_slot, r):
        token = jnp.where(r < count, ids_ref[0, 0, r], n_tok + r)
        dst = pl.multiple_of(token * tile, tile)
        return pltpu.make_async_copy(obuf.at[src_slot, pl.ds(r * tile, tile)], out_hbm.at[pl.ds(dst, tile)],
                                     sem_out.at[src_slot])

    def wait_gather(dst_slot):
        pltpu.make_async_copy(x_hbm.at[pl.ds(0, rows * tile)], xbuf.at[dst_slot], sem_in.at[dst_slot]).wait()

    def wait_scatter(src_slot):
        pltpu.make_async_copy(obuf.at[src_slot], out_hbm.at[pl.ds(0, rows * tile)], sem_out.at[src_slot]).wait()

    def start_all(make_copy):
        lax.fori_loop(0, rows, lambda r, c: (make_copy(r).start(), c)[1], 0, unroll=8)

    @pl.when(blk == 0)
    def _():
        obuf[1] = jnp.zeros(obuf.shape[1:], obuf.dtype)
        start_all(functools.partial(gather_row, tok_ref, 0))

    wait_gather(slot)

    def start_neighbours(r):
        gather_row(tok_next_ref, other, r).start()
        return scatter_row(tok_prev_ref, n_valid_prev, other, r)

    start_all(start_neighbours)

    @pl.when(n_valid > 0)
    def _():
        x = _load_rows(xbuf.at[slot], rows)
        xb = x.astype(jnp.bfloat16)

        def up(w1_ref, b1_ref):
            return jnp.dot(xb, w1_ref[0], preferred_element_type=jnp.float32) + b1_ref[0]

        def down(h, w2_ref, b2_ref):
            h = 0.5 * h * (1.0 + lax.erf(h * (2.0 ** -0.5)))
            return jnp.dot(h.astype(jnp.bfloat16), w2_ref[0], preferred_element_type=jnp.float32) + b2_ref[0]

        h_a = up(w1a_ref, b1a_ref)
        h_b = up(w1b_ref, b1b_ref)

        ea = ea_ref[blk]
        eb = eb_ref[blk]
        w_diff = wr_ref[pl.ds(eb, 1), :] - wr_ref[pl.ds(ea, 1), :]
        delta = jnp.sum(x * w_diff, axis=-1, keepdims=True) + (br_ref[eb] - br_ref[ea])
        gate_a = 1.0 / (1.0 + jnp.exp(delta))
        gate_b = 1.0 / (1.0 + jnp.exp(-delta))
        y = gate_a * down(h_a, w2a_ref, b2a_ref) + gate_b * down(h_b, w2b_ref, b2b_ref)
        _store_rows(obuf.at[slot], _layer_norm(alpha * x + y, lng_ref[...], lnb_ref[...]))

    wait_scatter(other)

    @pl.when(blk == last)
    def _():
        wait_gather(other)
        start_all(functools.partial(scatter_row, tok_ref, n_valid, slot))
        wait_scatter(slot)


def _ffn_call(x1, t, buf_tok, block_ea, block_eb, block_nvalid, lw, alpha):
    n_blocks = block_ea.shape[0]
    _, d, d_ff = lw["w1"].shape
    const = lambda shape: pl.BlockSpec(shape, lambda b, ea, eb, nv: (0,) * len(shape))
    by_a = lambda shape: pl.BlockSpec(shape, lambda b, ea, eb, nv: (ea[b], 0, 0))
    by_b = lambda shape: pl.BlockSpec(shape, lambda b, ea, eb, nv: (eb[b], 0, 0))
    ids = lambda shift: pl.BlockSpec(
        (1, 1, MOE_BLOCK), lambda b, ea, eb, nv: (jnp.clip(b + shift, 0, n_blocks - 1), 0, 0),
        memory_space=pltpu.SMEM)
    grid_spec = pltpu.PrefetchScalarGridSpec(
        num_scalar_prefetch=3,
        grid=(n_blocks,),
        in_specs=[
            ids(0), ids(1), ids(-1),
            pl.BlockSpec(memory_space=pl.ANY),
            by_a((1, d, d_ff)), by_a((1, 1, d_ff)), by_a((1, d_ff, d)), by_a((1, 1, d)),
            by_b((1, d, d_ff)), by_b((1, 1, d_ff)), by_b((1, d_ff, d)), by_b((1, 1, d)),
            const((N_EXPERTS, d)),
            pl.BlockSpec(memory_space=pltpu.SMEM),
            const((1, d)), const((1, d)),
        ],
        out_specs=pl.BlockSpec(memory_space=pl.ANY),
        scratch_shapes=[pltpu.VMEM((2, MOE_BLOCK * ROW_CHUNKS, LANES), jnp.float32),
                        pltpu.VMEM((2, MOE_BLOCK * ROW_CHUNKS, LANES), jnp.float32),
                        pltpu.SemaphoreType.DMA((2,)), pltpu.SemaphoreType.DMA((2,))],
    )
    return pl.pallas_call(
        functools.partial(_ffn_kernel, alpha=alpha, n_tok=t),
        grid_spec=grid_spec,
        out_shape=jax.ShapeDtypeStruct(((t + MOE_BLOCK) * ROW_CHUNKS, LANES), jnp.float32),
        compiler_params=_params(("arbitrary",)),
        name="ffn",
    )(block_ea, block_eb, block_nvalid, buf_tok, buf_tok, buf_tok, x1,
      lw["w1"], lw["b1"], lw["w2"], lw["b2"], lw["w1"], lw["b1"], lw["w2"], lw["b2"],
      lw["w_router_rows"], lw["b_router"], lw["ln2_g"], lw["ln2_b"])


def _dispatch_tables(cls, tile_counts, t):
    counts = jnp.sum(tile_counts, axis=0)[:N_CLASSES, 0]
    n_blocks = -(-(t + N_CLASSES * (MOE_BLOCK - 1)) // MOE_BLOCK)
    padded = (counts + MOE_BLOCK - 1) // MOE_BLOCK * MOE_BLOCK
    pad_ends = jnp.cumsum(padded)
    pad_starts = pad_ends - padded
    class_start = jnp.zeros((CLASS_ROWS, 1), jnp.float32).at[:N_CLASSES, 0].set(pad_starts.astype(jnp.float32))
    dest = _rank_call(cls, class_start)
    n_slots = n_blocks * MOE_BLOCK
    pad_lo = jnp.concatenate([pad_starts + counts, pad_ends[-1:]]).astype(jnp.int32)
    pad_hi = jnp.concatenate([pad_ends, jnp.full((1,), n_slots, pad_ends.dtype)]).astype(jnp.int32)
    buf_tok = _slot_table_call(dest, pad_lo, pad_hi, n_slots)
    block_start = jnp.arange(n_blocks, dtype=jnp.int32) * MOE_BLOCK
    block_cls = jnp.minimum(jnp.sum(pad_ends[None, :] <= block_start[:, None], axis=1), N_CLASSES - 1)
    block_cls = block_cls.astype(jnp.int32)
    n_valid = jnp.clip(counts[block_cls] - (block_start - pad_starts[block_cls]), 0, MOE_BLOCK)
    pair = jnp.asarray(PAIRS, jnp.int32)[block_cls % len(PAIRS)]
    group = block_cls // len(PAIRS)
    ea = group * EXPERTS_PER_GROUP + pair[:, 0]
    eb = group * EXPERTS_PER_GROUP + pair[:, 1]
    return (buf_tok.reshape(n_blocks, 1, MOE_BLOCK), ea.astype(jnp.int32), eb.astype(jnp.int32),
            n_valid.astype(jnp.int32))


def kernel(x_prompt, x_sample, ln_in_g, ln_in_b, w_in, b_in, rpb, w_pool, b_pool, pool_scale, w_oa, w_op,
           w_out, b_out, ln1_g, ln1_b, w_router, b_router, w1, b1, w2, b2, ln2_g, ln2_b):
    depth, d, _ = w_in.shape
    attn_dim = N_HEADS * HEAD_DIM
    pool_dim = w_op.shape[1]
    qkvp = 3 * attn_dim + pool_dim
    alpha = (2 * depth) ** 0.25
    bf = jnp.bfloat16
    f32 = jnp.float32

    wr_hi = w_router.T.astype(bf)
    wr_lo = (w_router.T - wr_hi.astype(f32)).astype(bf)
    shared = {
        "w_router_t": jnp.concatenate([wr_hi, wr_lo], axis=0),
        "b_router_t": b_router.reshape(N_EXPERTS, 1).astype(f32),
        "w_router_rows": w_router.T.astype(f32),
        "b_router": b_router.astype(f32),
    }
    layers = []
    for l in range(depth):
        lw = dict(shared)
        lw.update({
            "w_qkp": jnp.concatenate([w_in[l, :, :2 * attn_dim], w_in[l, :, 3 * attn_dim:qkvp]],
                                     axis=1).astype(bf),
            "b_qkp": jnp.concatenate([b_in[l, :2 * attn_dim], b_in[l, 3 * attn_dim:qkvp]]).reshape(1, -1),
            "w_v_t": w_in[l, :, 2 * attn_dim:3 * attn_dim].T.astype(bf),
            "b_v_t": b_in[l, 2 * attn_dim:3 * attn_dim].reshape(-1, 1),
            "w_gate": (0.5 * w_in[l, :, qkvp:]).astype(bf), "b_gate": 0.5 * b_in[l, qkvp:].reshape(1, -1),
            "table": _bias_table(rpb[l]),
            "w_pool": w_pool[l].astype(bf), "b_pool": b_pool[l].reshape(1, -1),
            "pool_scale": pool_scale[l].reshape(1, -1),
            "w_oa": w_oa[l].astype(bf), "w_op": w_op[l].astype(bf), "w_out": (0.5 * w_out[l]).astype(bf),
            "b_out": b_out[l].reshape(1, -1),
            "ln1_g": ln1_g[l].reshape(1, -1), "ln1_b": ln1_b[l].reshape(1, -1),
            "w1": w1[l].astype(bf), "b1": b1[l].reshape(N_EXPERTS, 1, -1),
            "w2": w2[l].astype(bf), "b2": b2[l].reshape(N_EXPERTS, 1, -1),
            "ln2_g": ln2_g[l].reshape(1, -1), "ln2_b": ln2_b[l].reshape(1, -1),
        })
        layers.append(lw)

    def run(x3):
        batch, seq, _ = x3.shape
        t = batch * seq
        x = _ln_call(x3.reshape(t, d), ln_in_g.reshape(1, -1), ln_in_b.reshape(1, -1))
        for lw in layers:
            q, k, vt, p = _proj_call(x, t, lw, attn_dim, pool_dim)
            a = _attn_call(q, k, vt, lw["table"], batch, seq)
            x1, cls, tile_counts = _mix_call(x, a, p, lw, seq, alpha)
            buf_tok, ea, eb, n_valid = _dispatch_tables(cls, tile_counts, t)
            x = _ffn_call(x1, t, buf_tok, ea, eb, n_valid, lw, alpha)
        return _unpack_call(x, t).reshape(batch, seq, d)

    return (run(x_prompt), run(x_sample))
```

```python
import functools

import jax
import jax.numpy as jnp
from jax import lax
from jax.experimental import pallas as pl
from jax.experimental.pallas import tpu as pltpu

GRID_W = 64
N_HEADS = 8
HEAD_DIM = 64
WIN_ROWS = 8
WIN_COLS = 16
POOL_WINDOWS = (2, 4, 8, 16)
N_EXPERTS = 16
N_GROUPS = 4
EXPERTS_PER_GROUP = 4
PAIRS = ((0, 1), (0, 2), (0, 3), (1, 3), (1, 2), (3, 2))
N_CLASSES = N_GROUPS * len(PAIRS)
CLASS_ROWS = 32
MOE_BLOCK = 256
LN_EPS = 1e-5
NEG = -1e30

LANES = 128
ROW_CHUNKS = 8
LOG2_E = 1.4426950408889634
Q_ROWS = 4
KV_ROWS = 12
KV_BLK_ROWS = 4
TOK_TILE = 512
MIX_SUBTILES = 1
POOL_HALO = 8
VMEM_LIMIT = 56 * 1024 * 1024


def _layer_norm(x, g, b):
    mu = jnp.mean(x, axis=-1, keepdims=True)
    xc = x - mu
    var = jnp.mean(xc * xc, axis=-1, keepdims=True)
    return xc * lax.rsqrt(var + LN_EPS) * g + b


def _params(semantics):
    return pltpu.CompilerParams(dimension_semantics=semantics, vmem_limit_bytes=VMEM_LIMIT)


def _const_spec(shape):
    nd = len(shape)
    return pl.BlockSpec(shape, lambda *_: (0,) * nd)


def _load_rows(ref, n_tok, tok0=0):
    return jnp.concatenate([ref[pl.ds(tok0 * ROW_CHUNKS + s, n_tok, stride=ROW_CHUNKS), :]
                            for s in range(ROW_CHUNKS)], axis=1)


def _store_rows(ref, value, tok0=0):
    n_tok = value.shape[0]
    for s in range(ROW_CHUNKS):
        ref[pl.ds(tok0 * ROW_CHUNKS + s, n_tok, stride=ROW_CHUNKS), :] = value[:, s * LANES:(s + 1) * LANES]


def _tok_spec(tm):
    return pl.BlockSpec((tm * ROW_CHUNKS, LANES), lambda i: (i, 0))


def _ln_kernel(x_ref, g_ref, b_ref, o_ref):
    _store_rows(o_ref, _layer_norm(x_ref[...], g_ref[...], b_ref[...]))


def _ln_call(x, g, b):
    t, d = x.shape
    assert d == ROW_CHUNKS * LANES
    tm = TOK_TILE
    return pl.pallas_call(
        _ln_kernel,
        grid=(t // tm,),
        in_specs=[pl.BlockSpec((tm, d), lambda i: (i, 0)), _const_spec((1, d)), _const_spec((1, d))],
        out_specs=_tok_spec(tm),
        out_shape=jax.ShapeDtypeStruct((t * ROW_CHUNKS, LANES), jnp.float32),
        compiler_params=_params(("parallel",)),
        name="ln_in",
    )(x, g, b)


def _unpack_kernel(x_ref, o_ref):
    o_ref[...] = _load_rows(x_ref, o_ref.shape[0])


def _unpack_call(x, t):
    tm = TOK_TILE
    d = ROW_CHUNKS * LANES
    return pl.pallas_call(
        _unpack_kernel,
        grid=(t // tm,),
        in_specs=[_tok_spec(tm)],
        out_specs=pl.BlockSpec((tm, d), lambda i: (i, 0)),
        out_shape=jax.ShapeDtypeStruct((t, d), jnp.float32),
        compiler_params=_params(("parallel",)),
        name="unpack",
    )(x)


def _proj_kernel(x_ref, w_ref, b_ref, wvt_ref, bvt_ref, q_ref, k_ref, vt_ref, p_ref, *, attn_dim, scale):
    xb = _load_rows(x_ref, q_ref.shape[0]).astype(jnp.bfloat16)
    acc = jnp.dot(xb, w_ref[...], preferred_element_type=jnp.float32) + b_ref[...]
    q_ref[...] = (acc[:, :attn_dim] * scale).astype(jnp.bfloat16)
    k_ref[...] = acc[:, attn_dim:2 * attn_dim].astype(jnp.bfloat16)
    p_ref[...] = acc[:, 2 * attn_dim:]
    vt = lax.dot_general(wvt_ref[...], xb, (((1,), (1,)), ((), ())), preferred_element_type=jnp.float32)
    vt_ref[...] = (vt + bvt_ref[...]).astype(jnp.bfloat16)


def _proj_call(x, t, lw, attn_dim, pool_dim):
    w, b, wvt, bvt = lw["w_qkp"], lw["b_qkp"], lw["w_v_t"], lw["b_v_t"]
    d, n = w.shape
    tm = TOK_TILE
    row = lambda i: (i, 0)
    return pl.pallas_call(
        functools.partial(_proj_kernel, attn_dim=attn_dim, scale=HEAD_DIM ** -0.5 * LOG2_E),
        grid=(t // tm,),
        in_specs=[_tok_spec(tm), _const_spec((d, n)), _const_spec((1, n)), _const_spec(wvt.shape),
                  _const_spec(bvt.shape)],
        out_specs=[pl.BlockSpec((tm, attn_dim), row), pl.BlockSpec((tm, attn_dim), row),
                   pl.BlockSpec((attn_dim, tm), lambda i: (0, i)), pl.BlockSpec((tm, pool_dim), row)],
        out_shape=[jax.ShapeDtypeStruct((t, attn_dim), jnp.bfloat16),
                   jax.ShapeDtypeStruct((t, attn_dim), jnp.bfloat16),
                   jax.ShapeDtypeStruct((attn_dim, t), jnp.bfloat16),
                   jax.ShapeDtypeStruct((t, pool_dim), jnp.float32)],
        compiler_params=_params(("parallel",)),
        name="proj",
    )(x, w, b, wvt, bvt)


def _attn_kernel(q_ref, k0, k1, k2, vt0, vt1, vt2, tab_ref, o_ref):
    q = q_ref[...]
    k = jnp.concatenate([k0[...], k1[...], k2[...]], axis=0)
    vt = jnp.concatenate([vt0[...], vt1[...], vt2[...]], axis=1)
    lane_head = lax.broadcasted_iota(jnp.int32, (1, LANES), 1) // HEAD_DIM
    heads_per_vreg = LANES // HEAD_DIM

    def scores(h):
        group = slice(h // heads_per_vreg * LANES, (h // heads_per_vreg + 1) * LANES)
        q_h = jnp.where(lane_head == h % heads_per_vreg, q[:, group], jnp.zeros((), q.dtype))
        s = lax.dot_general(k[:, group], q_h, (((1,), (1,)), ((), ())),
                            preferred_element_type=jnp.float32)
        return s + tab_ref[0, h]

    def weighted_values(h, e, denom):
        rows_h = slice(h * HEAD_DIM, (h + 1) * HEAD_DIM)
        o = jnp.dot(vt[rows_h, :], e, preferred_element_type=jnp.float32)
        o_ref[rows_h, :] = (o / denom).astype(o_ref.dtype)

    s_next = scores(0)
    pending = None
    for h in range(N_HEADS):
        s = s_next
        if h + 1 < N_HEADS:
            s_next = scores(h + 1)
        m = jnp.max(s, axis=0, keepdims=True)
        e = jnp.exp2(s - m)
        denom = jnp.sum(e, axis=0, keepdims=True)
        if pending is not None:
            weighted_values(*pending)
        pending = (h, e.astype(jnp.bfloat16), denom)
    weighted_values(*pending)


def _attn_call(q, k, vt, table, batch, seq):
    t, attn_dim = q.shape
    rows = seq // GRID_W
    assert rows % Q_ROWS == 0 and rows >= WIN_ROWS
    assert Q_ROWS == KV_BLK_ROWS == WIN_ROWS // 2 and KV_ROWS == 3 * KV_BLK_ROWS
    nq = Q_ROWS * GRID_W
    q_blocks = seq // nq
    kv_block = lambda i, b, j: b * q_blocks + jnp.clip(i - 1 + j, 0, q_blocks - 1)

    def table_index(i, b):
        return (jnp.where(i == 0, 0, jnp.where(i == q_blocks - 1, 2, 1)), 0, 0, 0)

    n_kv = KV_ROWS // KV_BLK_ROWS
    k_specs = [pl.BlockSpec((nq, attn_dim), functools.partial(lambda i, b, j: (kv_block(i, b, j), 0), j=j))
               for j in range(n_kv)]
    vt_specs = [pl.BlockSpec((attn_dim, nq), functools.partial(lambda i, b, j: (0, kv_block(i, b, j)), j=j))
                for j in range(n_kv)]
    return pl.pallas_call(
        _attn_kernel,
        grid=(q_blocks, batch),
        in_specs=[pl.BlockSpec((nq, attn_dim), lambda i, b: (b * q_blocks + i, 0))] + k_specs + vt_specs
        + [pl.BlockSpec((1,) + table.shape[1:], table_index)],
        out_specs=pl.BlockSpec((attn_dim, nq), lambda i, b: (0, b * q_blocks + i)),
        out_shape=jax.ShapeDtypeStruct((attn_dim, t), jnp.bfloat16),
        compiler_params=_params(("parallel", "parallel")),
        name="attn",
    )(q, k, k, k, vt, vt, vt, table)


def _bias_table(rpb):
    n_dr = 2 * WIN_ROWS - 1
    n_dc = 2 * WIN_COLS - 1
    n_heads = rpb.shape[0]
    c = jnp.arange(GRID_W)
    col_start = jnp.clip(c - WIN_COLS // 2, 0, GRID_W - WIN_COLS)
    col_ok = (c[None, :] >= col_start[:, None]) & (c[None, :] < col_start[:, None] + WIN_COLS)
    dc = c[None, :] - c[:, None] + (WIN_COLS - 1)
    onehot = ((dc[None] == jnp.arange(n_dc)[:, None, None]) & col_ok[None]).astype(jnp.float32)
    bank = jnp.einsum("hrd,dck->hrck", rpb.astype(jnp.float32) * LOG2_E, onehot,
                      precision=lax.Precision.HIGHEST)
    lead = (KV_ROWS - Q_ROWS) // 2
    i = jnp.arange(Q_ROWS)[:, None]
    j = jnp.arange(KV_ROWS)[None, :]
    row_ok = jnp.stack([(j >= lead) & (j < lead + WIN_ROWS) & (i >= 0),
                        (j - i >= 0) & (j - i < WIN_ROWS),
                        (j < WIN_ROWS) & (i >= 0)])
    first = [WIN_ROWS - 1 - lead - qi for qi in range(Q_ROWS)]
    assert min(first) >= 0 and max(first) + KV_ROWS <= n_dr
    tiles = jnp.stack([bank[:, f:f + KV_ROWS] for f in first], axis=1)
    ok = row_ok[:, None, :, :, None, None] & col_ok[None, None, None, None]
    tab = jnp.where(ok, tiles[None], NEG)
    tab = jnp.transpose(tab, (0, 1, 3, 5, 2, 4))
    return tab.reshape(3, n_heads, KV_ROWS * GRID_W, Q_ROWS * GRID_W)


def _route_class(logits_t):
    m = jnp.max(logits_t, axis=0, keepdims=True)
    e = jnp.exp(logits_t - m)
    probs = e / jnp.sum(e, axis=0, keepdims=True)
    p = [probs[r:r + 1, :] for r in range(N_EXPERTS)]

    def top2_sum(a, b, c, d):
        hi1, lo1 = jnp.maximum(a, b), jnp.minimum(a, b)
        hi2, lo2 = jnp.maximum(c, d), jnp.minimum(c, d)
        return jnp.maximum(hi1, hi2) + jnp.maximum(jnp.minimum(hi1, hi2), jnp.maximum(lo1, lo2))

    score = [top2_sum(*p[4 * g:4 * g + 4]) for g in range(N_GROUPS)]
    best_g = jnp.zeros_like(score[0], dtype=jnp.int32)
    best_s = score[0]
    for g in range(1, N_GROUPS):
        upd = score[g] > best_s
        best_g = jnp.where(upd, g, best_g)
        best_s = jnp.where(upd, score[g], best_s)

    vals = []
    for r in range(EXPERTS_PER_GROUP):
        val = p[r]
        for g in range(1, N_GROUPS):
            val = jnp.where(best_g == g, p[4 * g + r], val)
        vals.append(val)

    def argmax4(vs):
        idx = jnp.zeros_like(best_g)
        top = vs[0]
        for r in range(1, EXPERTS_PER_GROUP):
            upd = vs[r] > top
            idx = jnp.where(upd, r, idx)
            top = jnp.where(upd, vs[r], top)
        return idx

    i1 = argmax4(vals)
    i2 = argmax4([jnp.where(i1 == r, -1.0, vals[r]) for r in range(EXPERTS_PER_GROUP)])
    lo = jnp.minimum(i1, i2)
    hi = jnp.maximum(i1, i2)
    pair = jnp.zeros_like(lo)
    for n, (a, b) in enumerate(PAIRS):
        pair = jnp.where((lo == min(a, b)) & (hi == max(a, b)), n, pair)
    return best_g * len(PAIRS) + pair


def _mix_kernel(x_ref, a_ref, p_ref, pprev_ref, pnext_ref, wg_ref, bg_ref, wpool_ref, bpool_ref,
                pscale_ref, woa_ref, wop_ref, wout_ref, bout_ref, lng_ref, lnb_ref, wr_ref, br_ref,
                x1_ref, cls_ref, cnt_ref, halo_ref, *, seq, alpha):
    tm = a_ref.shape[1]
    ts = tm // MIX_SUBTILES
    d = wout_ref.shape[1]
    pool_dim = p_ref.shape[1]
    group_dim = pool_dim // len(POOL_WINDOWS)
    tiles_per_seq = seq // tm
    tile_in_seq = pl.program_id(0) % tiles_per_seq
    at_start = tile_in_seq == 0
    at_end = tile_in_seq == tiles_per_seq - 1

    halo_ref[0:POOL_HALO, :] = jnp.where(at_start, 0.0, pprev_ref[...])
    halo_ref[POOL_HALO:POOL_HALO + tm, :] = p_ref[...]
    halo_ref[POOL_HALO + tm:, :] = jnp.where(at_end, 0.0, pnext_ref[...])

    def projections(r0):
        x = _load_rows(x_ref, ts, r0)
        half_z = jnp.dot(x.astype(jnp.bfloat16), wg_ref[...], preferred_element_type=jnp.float32) + bg_ref[...]
        attn_out = lax.dot_general(a_ref[:, r0:r0 + ts], woa_ref[...], (((0,), (0,)), ((), ())),
                                   preferred_element_type=jnp.float32)
        return x, half_z, attn_out

    def window_diffs(r0):
        pos = tile_in_seq * tm + r0 + lax.broadcasted_iota(jnp.int32, (ts, 1), 0)
        base = POOL_HALO + r0
        diffs = []
        for g, w in enumerate(POOL_WINDOWS):
            lanes = slice(g * group_dim, (g + 1) * group_dim)
            total = halo_ref[base - w // 2:base - w // 2 + ts, lanes]
            for off in range(-w // 2 + 1, w // 2):
                total = total + halo_ref[base + off:base + off + ts, lanes]
            count = jnp.clip(pos + w // 2, 0, seq) - jnp.clip(pos - w // 2, 0, seq)
            diff = total / count.astype(jnp.float32) - halo_ref[base:base + ts, lanes]
            diffs.append(diff.astype(jnp.bfloat16))
        return diffs

    def pool_mix(diffs):
        pooled = []
        for g, diff in enumerate(diffs):
            lanes = slice(g * group_dim, (g + 1) * group_dim)
            y = jnp.dot(diff, wpool_ref[g], preferred_element_type=jnp.float32)
            pooled.append((y + bpool_ref[:, lanes]) * pscale_ref[:, lanes])
        return jnp.concatenate(pooled, axis=1).astype(jnp.bfloat16)

    def pool_project(pm):
        return jnp.dot(pm, wop_ref[...], preferred_element_type=jnp.float32)

    def output_project(half_z, attn_out, pool_out):
        gates2 = jnp.tanh(half_z) + 1.0
        mix2 = gates2[:, :d] * attn_out + gates2[:, d:] * pool_out
        return jnp.dot(mix2.astype(jnp.bfloat16), wout_ref[...], preferred_element_type=jnp.float32) + bout_ref[...]

    def finish(r0, x, y):
        x1 = _layer_norm(alpha * x + y, lng_ref[...], lnb_ref[...])
        _store_rows(x1_ref, x1, r0)
        x_hi = x1.astype(jnp.bfloat16)
        x_lo = (x1 - x_hi.astype(jnp.float32)).astype(jnp.bfloat16)
        nt = (((1,), (1,)), ((), ()))
        both = lax.dot_general(wr_ref[...], x_hi, nt, preferred_element_type=jnp.float32)
        cross = lax.dot_general(wr_ref[0:N_EXPERTS, :], x_lo, nt, preferred_element_type=jnp.float32)
        logits_t = both[0:N_EXPERTS, :] + both[N_EXPERTS:, :] + cross + br_ref[...]
        cls_ref[0, :, r0:r0 + ts] = _route_class(logits_t)

    starts = [n * ts for n in range(MIX_SUBTILES)]
    diffs = [window_diffs(r0) for r0 in starts]
    staged = [projections(r0) for r0 in starts]
    pooled = [pool_mix(df) for df in diffs]
    pool_outs = [pool_project(pm) for pm in pooled]
    ys = [output_project(half_z, attn_out, po) for (_, half_z, attn_out), po in zip(staged, pool_outs)]
    for r0, (x, _, _), y in zip(starts, staged, ys):
        finish(r0, x, y)

    onehot = lax.broadcasted_iota(jnp.int32, (CLASS_ROWS, tm), 0) == cls_ref[0]
    cnt_ref[0] = jnp.sum(onehot.astype(jnp.float32), axis=1, keepdims=True).astype(jnp.int32)


def _mix_call(x, a, p, lw, seq, alpha):
    attn_dim, t = a.shape
    tm = TOK_TILE
    assert seq % tm == 0 and tm % POOL_HALO == 0
    pool_dim = p.shape[1]
    row = lambda i: (i, 0)
    halo_blocks = t // POOL_HALO
    per_tile = tm // POOL_HALO
    prev = lambda i: (jnp.maximum(i * per_tile - 1, 0), 0)
    nxt = lambda i: (jnp.minimum((i + 1) * per_tile, halo_blocks - 1), 0)
    weights = [lw["w_gate"], lw["b_gate"], lw["w_pool"], lw["b_pool"], lw["pool_scale"], lw["w_oa"],
               lw["w_op"], lw["w_out"], lw["b_out"], lw["ln1_g"], lw["ln1_b"], lw["w_router_t"],
               lw["b_router_t"]]
    return pl.pallas_call(
        functools.partial(_mix_kernel, seq=seq, alpha=alpha),
        grid=(t // tm,),
        in_specs=[_tok_spec(tm), pl.BlockSpec((attn_dim, tm), lambda i: (0, i)),
                  pl.BlockSpec((tm, pool_dim), row), pl.BlockSpec((POOL_HALO, pool_dim), prev),
                  pl.BlockSpec((POOL_HALO, pool_dim), nxt)] + [_const_spec(w.shape) for w in weights],
        out_specs=[_tok_spec(tm), pl.BlockSpec((1, 1, tm), lambda i: (i, 0, 0)),
                   pl.BlockSpec((1, CLASS_ROWS, 1), lambda i: (i, 0, 0))],
        out_shape=[jax.ShapeDtypeStruct((t * ROW_CHUNKS, LANES), jnp.float32),
                   jax.ShapeDtypeStruct((t // tm, 1, tm), jnp.int32),
                   jax.ShapeDtypeStruct((t // tm, CLASS_ROWS, 1), jnp.int32)],
        scratch_shapes=[pltpu.VMEM((tm + 2 * POOL_HALO, pool_dim), jnp.float32)],
        compiler_params=_params(("parallel",)),
        name="mix",
    )(x, a, p, p, p, *weights)


def _rank_kernel(cls_ref, start_ref, dest_ref, carry_ref):
    tm = cls_ref.shape[2]

    @pl.when(pl.program_id(0) == 0)
    def _():
        carry_ref[...] = start_ref[...]

    cls = cls_ref[0]
    onehot = lax.broadcasted_iota(jnp.int32, (CLASS_ROWS, tm), 0) == cls
    upper = (lax.broadcasted_iota(jnp.int32, (tm, tm), 0)
             <= lax.broadcasted_iota(jnp.int32, (tm, tm), 1)).astype(jnp.bfloat16)
    incl = jnp.dot(onehot.astype(jnp.bfloat16), upper, preferred_element_type=jnp.float32)
    slot = incl - 1.0 + carry_ref[...]
    dest_ref[0] = jnp.sum(jnp.where(onehot, slot, 0.0), axis=0, keepdims=True).astype(jnp.int32)
    carry_ref[...] = carry_ref[...] + incl[:, tm - 1:tm]


def _rank_call(cls, class_start):
    nt, _, tm = cls.shape
    return pl.pallas_call(
        _rank_kernel,
        grid=(nt,),
        in_specs=[pl.BlockSpec((1, 1, tm), lambda i: (i, 0, 0)), _const_spec((CLASS_ROWS, 1))],
        out_specs=pl.BlockSpec((1, 1, tm), lambda i: (i, 0, 0)),
        out_shape=jax.ShapeDtypeStruct((nt, 1, tm), jnp.int32),
        scratch_shapes=[pltpu.VMEM((CLASS_ROWS, 1), jnp.float32)],
        compiler_params=_params(("arbitrary",)),
        name="rank",
    )(cls, class_start)


def _slot_table_kernel(pad_lo_ref, pad_hi_ref, dest_ref, table_ref):
    i = pl.program_id(0)
    chunk = dest_ref.shape[2]

    @pl.when(i == 0)
    def _():
        def zero(s, c):
            table_ref[s] = 0
            return c
        for cls in range(N_CLASSES + 1):
            lax.fori_loop(pad_lo_ref[cls], pad_hi_ref[cls], zero, 0)

    def place(r, c):
        table_ref[dest_ref[0, 0, r]] = i * chunk + r
        return c
    lax.fori_loop(0, chunk, place, 0, unroll=8)


def _slot_table_call(dest, pad_lo, pad_hi, n_slots):
    n_chunks, _, chunk = dest.shape
    grid_spec = pltpu.PrefetchScalarGridSpec(
        num_scalar_prefetch=2,
        grid=(n_chunks,),
        in_specs=[pl.BlockSpec((1, 1, chunk), lambda i, lo, hi: (i, 0, 0), memory_space=pltpu.SMEM)],
        out_specs=pl.BlockSpec(memory_space=pltpu.SMEM),
    )
    return pl.pallas_call(
        _slot_table_kernel,
        grid_spec=grid_spec,
        out_shape=jax.ShapeDtypeStruct((n_slots,), jnp.int32),
        compiler_params=_params(("arbitrary",)),
        name="slot_table",
    )(pad_lo, pad_hi, dest)


def _ffn_kernel(ea_ref, eb_ref, nvalid_ref, tok_ref, tok_next_ref, tok_prev_ref, x_hbm,
                w1a_ref, b1a_ref, w2a_ref, b2a_ref, w1b_ref, b1b_ref, w2b_ref, b2b_ref,
                wr_ref, br_ref, lng_ref, lnb_ref, out_hbm, xbuf, obuf, sem_in, sem_out,
                *, alpha, n_tok):
    blk = pl.program_id(0)
    last = pl.num_programs(0) - 1
    slot = blk % 2
    other = 1 - slot
    n_valid = nvalid_ref[blk]
    n_valid_prev = jnp.where(blk > 0, nvalid_ref[jnp.maximum(blk - 1, 0)], 0)
    rows = MOE_BLOCK
    tile = ROW_CHUNKS

    def gather_row(ids_ref, dst_slot, r):
        src = pl.multiple_of(ids_ref[0, 0, r] * tile, tile)
        return pltpu.make_async_copy(x_hbm.at[pl.ds(src, tile)], xbuf.at[dst_slot, pl.ds(r * tile, tile)],
                                     sem_in.at[dst_slot])

    def scatter_row(ids_ref, count, src_slot, r):
        token = jnp.where(r < count, ids_ref[0, 0, r], n_tok + r)
        dst = pl.multiple_of(token * tile, tile)
        return pltpu.make_async_copy(obuf.at[src_slot, pl.ds(r * tile, tile)], out_hbm.at[pl.ds(dst, tile)],
                                     sem_out.at[src_slot])

    def wait_gather(dst_slot):
        pltpu.make_async_copy(x_hbm.at[pl.ds(0, rows * tile)], xbuf.at[dst_slot], sem_in.at[dst_slot]).wait()

    def wait_scatter(src_slot):
        pltpu.make_async_copy(obuf.at[src_slot], out_hbm.at[pl.ds(0, rows * tile)], sem_out.at[src_slot]).wait()

    def start_all(make_copy):
        lax.fori_loop(0, rows, lambda r, c: (make_copy(r).start(), c)[1], 0, unroll=8)

    @pl.when(blk == 0)
    def _():
        obuf[1] = jnp.zeros(obuf.shape[1:], obuf.dtype)
        start_all(functools.partial(gather_row, tok_ref, 0))

    wait_gather(slot)

    def start_neighbours(r):
        gather_row(tok_next_ref, other, r).start()
        return scatter_row(tok_prev_ref, n_valid_prev, other, r)

    start_all(start_neighbours)

    @pl.when(n_valid > 0)
    def _():
        x = _load_rows(xbuf.at[slot], rows)
        xb = x.astype(jnp.bfloat16)

        def up(w1_ref, b1_ref):
            return jnp.dot(xb, w1_ref[0], preferred_element_type=jnp.float32) + b1_ref[0]

        def down(h, w2_ref, b2_ref):
            h = 0.5 * h * (1.0 + lax.erf(h * (2.0 ** -0.5)))
            return jnp.dot(h.astype(jnp.bfloat16), w2_ref[0], preferred_element_type=jnp.float32) + b2_ref[0]

        h_a = up(w1a_ref, b1a_ref)
        h_b = up(w1b_ref, b1b_ref)

        ea = ea_ref[blk]
        eb = eb_ref[blk]
        w_diff = wr_ref[pl.ds(eb, 1), :] - wr_ref[pl.ds(ea, 1), :]
        delta = jnp.sum(x * w_diff, axis=-1, keepdims=True) + (br_ref[eb] - br_ref[ea])
        gate_a = 1.0 / (1.0 + jnp.exp(delta))
        gate_b = 1.0 / (1.0 + jnp.exp(-delta))
        y = gate_a * down(h_a, w2a_ref, b2a_ref) + gate_b * down(h_b, w2b_ref, b2b_ref)
        _store_rows(obuf.at[slot], _layer_norm(alpha * x + y, lng_ref[...], lnb_ref[...]))

    wait_scatter(other)

    @pl.when(blk == last)
    def _():
        wait_gather(other)
        start_all(functools.partial(scatter_row, tok_ref, n_valid, slot))
        wait_scatter(slot)


def _ffn_call(x1, t, buf_tok, block_ea, block_eb, block_nvalid, lw, alpha):
    n_blocks = block_ea.shape[0]
    _, d, d_ff = lw["w1"].shape
    const = lambda shape: pl.BlockSpec(shape, lambda b, ea, eb, nv: (0,) * len(shape))
    by_a = lambda shape: pl.BlockSpec(shape, lambda b, ea, eb, nv: (ea[b], 0, 0))
    by_b = lambda shape: pl.BlockSpec(shape, lambda b, ea, eb, nv: (eb[b], 0, 0))
    ids = lambda shift: pl.BlockSpec(
        (1, 1, MOE_BLOCK), lambda b, ea, eb, nv: (jnp.clip(b + shift, 0, n_blocks - 1), 0, 0),
        memory_space=pltpu.SMEM)
    grid_spec = pltpu.PrefetchScalarGridSpec(
        num_scalar_prefetch=3,
        grid=(n_blocks,),
        in_specs=[
            ids(0), ids(1), ids(-1),
            pl.BlockSpec(memory_space=pl.ANY),
            by_a((1, d, d_ff)), by_a((1, 1, d_ff)), by_a((1, d_ff, d)), by_a((1, 1, d)),
            by_b((1, d, d_ff)), by_b((1, 1, d_ff)), by_b((1, d_ff, d)), by_b((1, 1, d)),
            const((N_EXPERTS, d)),
            pl.BlockSpec(memory_space=pltpu.SMEM),
            const((1, d)), const((1, d)),
        ],
        out_specs=pl.BlockSpec(memory_space=pl.ANY),
        scratch_shapes=[pltpu.VMEM((2, MOE_BLOCK * ROW_CHUNKS, LANES), jnp.float32),
                        pltpu.VMEM((2, MOE_BLOCK * ROW_CHUNKS, LANES), jnp.float32),
                        pltpu.SemaphoreType.DMA((2,)), pltpu.SemaphoreType.DMA((2,))],
    )
    return pl.pallas_call(
        functools.partial(_ffn_kernel, alpha=alpha, n_tok=t),
        grid_spec=grid_spec,
        out_shape=jax.ShapeDtypeStruct(((t + MOE_BLOCK) * ROW_CHUNKS, LANES), jnp.float32),
        compiler_params=_params(("arbitrary",)),
        name="ffn",
    )(block_ea, block_eb, block_nvalid, buf_tok, buf_tok, buf_tok, x1,
      lw["w1"], lw["b1"], lw["w2"], lw["b2"], lw["w1"], lw["b1"], lw["w2"], lw["b2"],
      lw["w_router_rows"], lw["b_router"], lw["ln2_g"], lw["ln2_b"])


def _dispatch_tables(cls, tile_counts, t):
    counts = jnp.sum(tile_counts, axis=0)[:N_CLASSES, 0]
    n_blocks = -(-(t + N_CLASSES * (MOE_BLOCK - 1)) // MOE_BLOCK)
    padded = (counts + MOE_BLOCK - 1) // MOE_BLOCK * MOE_BLOCK
    pad_ends = jnp.cumsum(padded)
    pad_starts = pad_ends - padded
    class_start = jnp.zeros((CLASS_ROWS, 1), jnp.float32).at[:N_CLASSES, 0].set(pad_starts.astype(jnp.float32))
    dest = _rank_call(cls, class_start)
    n_slots = n_blocks * MOE_BLOCK
    pad_lo = jnp.concatenate([pad_starts + counts, pad_ends[-1:]]).astype(jnp.int32)
    pad_hi = jnp.concatenate([pad_ends, jnp.full((1,), n_slots, pad_ends.dtype)]).astype(jnp.int32)
    buf_tok = _slot_table_call(dest, pad_lo, pad_hi, n_slots)
    block_start = jnp.arange(n_blocks, dtype=jnp.int32) * MOE_BLOCK
    block_cls = jnp.minimum(jnp.sum(pad_ends[None, :] <= block_start[:, None], axis=1), N_CLASSES - 1)
    block_cls = block_cls.astype(jnp.int32)
    n_valid = jnp.clip(counts[block_cls] - (block_start - pad_starts[block_cls]), 0, MOE_BLOCK)
    pair = jnp.asarray(PAIRS, jnp.int32)[block_cls % len(PAIRS)]
    group = block_cls // len(PAIRS)
    ea = group * EXPERTS_PER_GROUP + pair[:, 0]
    eb = group * EXPERTS_PER_GROUP + pair[:, 1]
    return (buf_tok.reshape(n_blocks, 1, MOE_BLOCK), ea.astype(jnp.int32), eb.astype(jnp.int32),
            n_valid.astype(jnp.int32))


def kernel(x_prompt, x_sample, ln_in_g, ln_in_b, w_in, b_in, rpb, w_pool, b_pool, pool_scale, w_oa, w_op,
           w_out, b_out, ln1_g, ln1_b, w_router, b_router, w1, b1, w2, b2, ln2_g, ln2_b):
    depth, d, _ = w_in.shape
    attn_dim = N_HEADS * HEAD_DIM
    pool_dim = w_op.shape[1]
    qkvp = 3 * attn_dim + pool_dim
    alpha = (2 * depth) ** 0.25
    bf = jnp.bfloat16
    f32 = jnp.float32

    wr_hi = w_router.T.astype(bf)
    wr_lo = (w_router.T - wr_hi.astype(f32)).astype(bf)
    shared = {
        "w_router_t": jnp.concatenate([wr_hi, wr_lo], axis=0),
        "b_router_t": b_router.reshape(N_EXPERTS, 1).astype(f32),
        "w_router_rows": w_router.T.astype(f32),
        "b_router": b_router.astype(f32),
    }
    layers = []
    for l in range(depth):
        lw = dict(shared)
        lw.update({
            "w_qkp": jnp.concatenate([w_in[l, :, :2 * attn_dim], w_in[l, :, 3 * attn_dim:qkvp]],
                                     axis=1).astype(bf),
            "b_qkp": jnp.concatenate([b_in[l, :2 * attn_dim], b_in[l, 3 * attn_dim:qkvp]]).reshape(1, -1),
            "w_v_t": w_in[l, :, 2 * attn_dim:3 * attn_dim].T.astype(bf),
            "b_v_t": b_in[l, 2 * attn_dim:3 * attn_dim].reshape(-1, 1),
            "w_gate": (0.5 * w_in[l, :, qkvp:]).astype(bf), "b_gate": 0.5 * b_in[l, qkvp:].reshape(1, -1),
            "table": _bias_table(rpb[l]),
            "w_pool": w_pool[l].astype(bf), "b_pool": b_pool[l].reshape(1, -1),
            "pool_scale": pool_scale[l].reshape(1, -1),
            "w_oa": w_oa[l].astype(bf), "w_op": w_op[l].astype(bf), "w_out": (0.5 * w_out[l]).astype(bf),
            "b_out": b_out[l].reshape(1, -1),
            "ln1_g": ln1_g[l].reshape(1, -1), "ln1_b": ln1_b[l].reshape(1, -1),
            "w1": w1[l].astype(bf), "b1": b1[l].reshape(N_EXPERTS, 1, -1),
            "w2": w2[l].astype(bf), "b2": b2[l].reshape(N_EXPERTS, 1, -1),
            "ln2_g": ln2_g[l].reshape(1, -1), "ln2_b": ln2_b[l].reshape(1, -1),
        })
        layers.append(lw)

    def run(x3):
        batch, seq, _ = x3.shape
        t = batch * seq
        x = _ln_call(x3.reshape(t, d), ln_in_g.reshape(1, -1), ln_in_b.reshape(1, -1))
        for lw in layers:
            q, k, vt, p = _proj_call(x, t, lw, attn_dim, pool_dim)
            a = _attn_call(q, k, vt, lw["table"], batch, seq)
            x1, cls, tile_counts = _mix_call(x, a, p, lw, seq, alpha)
            buf_tok, ea, eb, n_valid = _dispatch_tables(cls, tile_counts, t)
            x = _ffn_call(x1, t, buf_tok, ea, eb, n_valid, lw, alpha)
        return _unpack_call(x, t).reshape(batch, seq, d)

    return (run(x_prompt), run(x_sample))
```

```python
import functools

import jax
import jax.numpy as jnp
from jax import lax
from jax.experimental import pallas as pl
from jax.experimental.pallas import tpu as pltpu

GRID_W = 64
N_HEADS = 8
HEAD_DIM = 64
WIN_ROWS = 8
WIN_COLS = 16
POOL_WINDOWS = (2, 4, 8, 16)
N_EXPERTS = 16
N_GROUPS = 4
EXPERTS_PER_GROUP = 4
PAIRS = ((0, 1), (0, 2), (0, 3), (1, 3), (1, 2), (3, 2))
N_CLASSES = N_GROUPS * len(PAIRS)
CLASS_ROWS = 32
MOE_BLOCK = 256
LN_EPS = 1e-5
NEG = -1e30

LANES = 128
ROW_CHUNKS = 8
LOG2_E = 1.4426950408889634
Q_ROWS = 4
KV_ROWS = 12
KV_BLK_ROWS = 4
TOK_TILE = 512
MIX_SUBTILES = 1
POOL_HALO = 8
VMEM_LIMIT = 56 * 1024 * 1024


def _layer_norm(x, g, b):
    mu = jnp.mean(x, axis=-1, keepdims=True)
    xc = x - mu
    var = jnp.mean(xc * xc, axis=-1, keepdims=True)
    return xc * lax.rsqrt(var + LN_EPS) * g + b


def _params(semantics):
    return pltpu.CompilerParams(dimension_semantics=semantics, vmem_limit_bytes=VMEM_LIMIT)


def _const_spec(shape):
    nd = len(shape)
    return pl.BlockSpec(shape, lambda *_: (0,) * nd)


def _load_rows(ref, n_tok, tok0=0):
    return jnp.concatenate([ref[pl.ds(tok0 * ROW_CHUNKS + s, n_tok, stride=ROW_CHUNKS), :]
                            for s in range(ROW_CHUNKS)], axis=1)


def _store_rows(ref, value, tok0=0):
    n_tok = value.shape[0]
    for s in range(ROW_CHUNKS):
        ref[pl.ds(tok0 * ROW_CHUNKS + s, n_tok, stride=ROW_CHUNKS), :] = value[:, s * LANES:(s + 1) * LANES]


def _tok_spec(tm):
    return pl.BlockSpec((tm * ROW_CHUNKS, LANES), lambda i: (i, 0))


def _ln_kernel(x_ref, g_ref, b_ref, o_ref):
    _store_rows(o_ref, _layer_norm(x_ref[...], g_ref[...], b_ref[...]))


def _ln_call(x, g, b):
    t, d = x.shape
    assert d == ROW_CHUNKS * LANES
    tm = TOK_TILE
    return pl.pallas_call(
        _ln_kernel,
        grid=(t // tm,),
        in_specs=[pl.BlockSpec((tm, d), lambda i: (i, 0)), _const_spec((1, d)), _const_spec((1, d))],
        out_specs=_tok_spec(tm),
        out_shape=jax.ShapeDtypeStruct((t * ROW_CHUNKS, LANES), jnp.float32),
        compiler_params=_params(("parallel",)),
        name="ln_in",
    )(x, g, b)


def _unpack_kernel(x_ref, o_ref):
    o_ref[...] = _load_rows(x_ref, o_ref.shape[0])


def _unpack_call(x, t):
    tm = TOK_TILE
    d = ROW_CHUNKS * LANES
    return pl.pallas_call(
        _unpack_kernel,
        grid=(t // tm,),
        in_specs=[_tok_spec(tm)],
        out_specs=pl.BlockSpec((tm, d), lambda i: (i, 0)),
        out_shape=jax.ShapeDtypeStruct((t, d), jnp.float32),
        compiler_params=_params(("parallel",)),
        name="unpack",
    )(x)


def _proj_kernel(x_ref, w_ref, b_ref, wvt_ref, bvt_ref, q_ref, k_ref, vt_ref, p_ref, *, attn_dim, scale):
    xb = _load_rows(x_ref, q_ref.shape[0]).astype(jnp.bfloat16)
    acc = jnp.dot(xb, w_ref[...], preferred_element_type=jnp.float32) + b_ref[...]
    q_ref[...] = (acc[:, :attn_dim] * scale).astype(jnp.bfloat16)
    k_ref[...] = acc[:, attn_dim:2 * attn_dim].astype(jnp.bfloat16)
    p_ref[...] = acc[:, 2 * attn_dim:]
    vt = lax.dot_general(wvt_ref[...], xb, (((1,), (1,)), ((), ())), preferred_element_type=jnp.float32)
    vt_ref[...] = (vt + bvt_ref[...]).astype(jnp.bfloat16)


def _proj_call(x, t, lw, attn_dim, pool_dim):
    w, b, wvt, bvt = lw["w_qkp"], lw["b_qkp"], lw["w_v_t"], lw["b_v_t"]
    d, n = w.shape
    tm = TOK_TILE
    row = lambda i: (i, 0)
    return pl.pallas_call(
        functools.partial(_proj_kernel, attn_dim=attn_dim, scale=HEAD_DIM ** -0.5 * LOG2_E),
        grid=(t // tm,),
        in_specs=[_tok_spec(tm), _const_spec((d, n)), _const_spec((1, n)), _const_spec(wvt.shape),
                  _const_spec(bvt.shape)],
        out_specs=[pl.BlockSpec((tm, attn_dim), row), pl.BlockSpec((tm, attn_dim), row),
                   pl.BlockSpec((attn_dim, tm), lambda i: (0, i)), pl.BlockSpec((tm, pool_dim), row)],
        out_shape=[jax.ShapeDtypeStruct((t, attn_dim), jnp.bfloat16),
                   jax.ShapeDtypeStruct((t, attn_dim), jnp.bfloat16),
                   jax.ShapeDtypeStruct((attn_dim, t), jnp.bfloat16),
                   jax.ShapeDtypeStruct((t, pool_dim), jnp.float32)],
        compiler_params=_params(("parallel",)),
        name="proj",
    )(x, w, b, wvt, bvt)


def _attn_kernel(q_ref, k0, k1, k2, vt0, vt1, vt2, tab_ref, o_ref):
    q = q_ref[...]
    k = jnp.concatenate([k0[...], k1[...], k2[...]], axis=0)
    vt = jnp.concatenate([vt0[...], vt1[...], vt2[...]], axis=1)
    lane_head = lax.broadcasted_iota(jnp.int32, (1, LANES), 1) // HEAD_DIM
    heads_per_vreg = LANES // HEAD_DIM

    def scores(h):
        group = slice(h // heads_per_vreg * LANES, (h // heads_per_vreg + 1) * LANES)
        q_h = jnp.where(lane_head == h % heads_per_vreg, q[:, group], jnp.zeros((), q.dtype))
        s = lax.dot_general(k[:, group], q_h, (((1,), (1,)), ((), ())),
                            preferred_element_type=jnp.float32)
        return s + tab_ref[0, h]

    def weighted_values(h, e, denom):
        rows_h = slice(h * HEAD_DIM, (h + 1) * HEAD_DIM)
        o = jnp.dot(vt[rows_h, :], e, preferred_element_type=jnp.float32)
        o_ref[rows_h, :] = (o / denom).astype(o_ref.dtype)

    s_next = scores(0)
    pending = None
    for h in range(N_HEADS):
        s = s_next
        if h + 1 < N_HEADS:
            s_next = scores(h + 1)
        m = jnp.max(s, axis=0, keepdims=True)
        e = jnp.exp2(s - m)
        denom = jnp.sum(e, axis=0, keepdims=True)
        if pending is not None:
            weighted_values(*pending)
        pending = (h, e.astype(jnp.bfloat16), denom)
    weighted_values(*pending)


def _attn_call(q, k, vt, table, batch, seq):
    t, attn_dim = q.shape
    rows = seq // GRID_W
    assert rows % Q_ROWS == 0 and rows >= WIN_ROWS
    assert Q_ROWS == KV_BLK_ROWS == WIN_ROWS // 2 and KV_ROWS == 3 * KV_BLK_ROWS
    nq = Q_ROWS * GRID_W
    q_blocks = seq // nq
    kv_block = lambda i, b, j: b * q_blocks + jnp.clip(i - 1 + j, 0, q_blocks - 1)

    def table_index(i, b):
        return (jnp.where(i == 0, 0, jnp.where(i == q_blocks - 1, 2, 1)), 0, 0, 0)

    n_kv = KV_ROWS // KV_BLK_ROWS
    k_specs = [pl.BlockSpec((nq, attn_dim), functools.partial(lambda i, b, j: (kv_block(i, b, j), 0), j=j))
               for j in range(n_kv)]
    vt_specs = [pl.BlockSpec((attn_dim, nq), functools.partial(lambda i, b, j: (0, kv_block(i, b, j)), j=j))
                for j in range(n_kv)]
    return pl.pallas_call(
        _attn_kernel,
        grid=(q_blocks, batch),
        in_specs=[pl.BlockSpec((nq, attn_dim), lambda i, b: (b * q_blocks + i, 0))] + k_specs + vt_specs
        + [pl.BlockSpec((1,) + table.shape[1:], table_index)],
        out_specs=pl.BlockSpec((attn_dim, nq), lambda i, b: (0, b * q_blocks + i)),
        out_shape=jax.ShapeDtypeStruct((attn_dim, t), jnp.bfloat16),
        compiler_params=_params(("parallel", "parallel")),
        name="attn",
    )(q, k, k, k, vt, vt, vt, table)


def _bias_table(rpb):
    n_dr = 2 * WIN_ROWS - 1
    n_dc = 2 * WIN_COLS - 1
    n_heads = rpb.shape[0]
    c = jnp.arange(GRID_W)
    col_start = jnp.clip(c - WIN_COLS // 2, 0, GRID_W - WIN_COLS)
    col_ok = (c[None, :] >= col_start[:, None]) & (c[None, :] < col_start[:, None] + WIN_COLS)
    dc = c[None, :] - c[:, None] + (WIN_COLS - 1)
    onehot = ((dc[None] == jnp.arange(n_dc)[:, None, None]) & col_ok[None]).astype(jnp.float32)
    bank = jnp.einsum("hrd,dck->hrck", rpb.astype(jnp.float32) * LOG2_E, onehot,
                      precision=lax.Precision.HIGHEST)
    lead = (KV_ROWS - Q_ROWS) // 2
    i = jnp.arange(Q_ROWS)[:, None]
    j = jnp.arange(KV_ROWS)[None, :]
    row_ok = jnp.stack([(j >= lead) & (j < lead + WIN_ROWS) & (i >= 0),
                        (j - i >= 0) & (j - i < WIN_ROWS),
                        (j < WIN_ROWS) & (i >= 0)])
    first = [WIN_ROWS - 1 - lead - qi for qi in range(Q_ROWS)]
    assert min(first) >= 0 and max(first) + KV_ROWS <= n_dr
    tiles = jnp.stack([bank[:, f:f + KV_ROWS] for f in first], axis=1)
    ok = row_ok[:, None, :, :, None, None] & col_ok[None, None, None, None]
    tab = jnp.where(ok, tiles[None], NEG)
    tab = jnp.transpose(tab, (0, 1, 3, 5, 2, 4))
    return tab.reshape(3, n_heads, KV_ROWS * GRID_W, Q_ROWS * GRID_W)


def _route_class(logits_t):
    m = jnp.max(logits_t, axis=0, keepdims=True)
    e = jnp.exp(logits_t - m)
    probs = e / jnp.sum(e, axis=0, keepdims=True)
    p = [probs[r:r + 1, :] for r in range(N_EXPERTS)]

    def top2_sum(a, b, c, d):
        hi1, lo1 = jnp.maximum(a, b), jnp.minimum(a, b)
        hi2, lo2 = jnp.maximum(c, d), jnp.minimum(c, d)
        return jnp.maximum(hi1, hi2) + jnp.maximum(jnp.minimum(hi1, hi2), jnp.maximum(lo1, lo2))

    score = [top2_sum(*p[4 * g:4 * g + 4]) for g in range(N_GROUPS)]
    best_g = jnp.zeros_like(score[0], dtype=jnp.int32)
    best_s = score[0]
    for g in range(1, N_GROUPS):
        upd = score[g] > best_s
        best_g = jnp.where(upd, g, best_g)
        best_s = jnp.where(upd, score[g], best_s)

    vals = []
    for r in range(EXPERTS_PER_GROUP):
        val = p[r]
        for g in range(1, N_GROUPS):
            val = jnp.where(best_g == g, p[4 * g + r], val)
        vals.append(val)

    def argmax4(vs):
        idx = jnp.zeros_like(best_g)
        top = vs[0]
        for r in range(1, EXPERTS_PER_GROUP):
            upd = vs[r] > top
            idx = jnp.where(upd, r, idx)
            top = jnp.where(upd, vs[r], top)
        return idx

    i1 = argmax4(vals)
    i2 = argmax4([jnp.where(i1 == r, -1.0, vals[r]) for r in range(EXPERTS_PER_GROUP)])
    lo = jnp.minimum(i1, i2)
    hi = jnp.maximum(i1, i2)
    pair = jnp.zeros_like(lo)
    for n, (a, b) in enumerate(PAIRS):
        pair = jnp.where((lo == min(a, b)) & (hi == max(a, b)), n, pair)
    return best_g * len(PAIRS) + pair


def _mix_kernel(x_ref, a_ref, p_ref, pprev_ref, pnext_ref, wg_ref, bg_ref, wpool_ref, bpool_ref,
                pscale_ref, woa_ref, wop_ref, wout_ref, bout_ref, lng_ref, lnb_ref, wr_ref, br_ref,
                x1_ref, cls_ref, cnt_ref, halo_ref, *, seq, alpha):
    tm = a_ref.shape[1]
    ts = tm // MIX_SUBTILES
    d = wout_ref.shape[1]
    pool_dim = p_ref.shape[1]
    group_dim = pool_dim // len(POOL_WINDOWS)
    tiles_per_seq = seq // tm
    tile_in_seq = pl.program_id(0) % tiles_per_seq
    at_start = tile_in_seq == 0
    at_end = tile_in_seq == tiles_per_seq - 1

    halo_ref[0:POOL_HALO, :] = jnp.where(at_start, 0.0, pprev_ref[...])
    halo_ref[POOL_HALO:POOL_HALO + tm, :] = p_ref[...]
    halo_ref[POOL_HALO + tm:, :] = jnp.where(at_end, 0.0, pnext_ref[...])

    def projections(r0):
        x = _load_rows(x_ref, ts, r0)
        half_z = jnp.dot(x.astype(jnp.bfloat16), wg_ref[...], preferred_element_type=jnp.float32) + bg_ref[...]
        attn_out = lax.dot_general(a_ref[:, r0:r0 + ts], woa_ref[...], (((0,), (0,)), ((), ())),
                                   preferred_element_type=jnp.float32)
        return x, half_z, attn_out

    def window_diffs(r0):
        pos = tile_in_seq * tm + r0 + lax.broadcasted_iota(jnp.int32, (ts, 1), 0)
        base = POOL_HALO + r0
        diffs = []
        for g, w in enumerate(POOL_WINDOWS):
            lanes = slice(g * group_dim, (g + 1) * group_dim)
            total = halo_ref[base - w // 2:base - w // 2 + ts, lanes]
            for off in range(-w // 2 + 1, w // 2):
                total = total + halo_ref[base + off:base + off + ts, lanes]
            count = jnp.clip(pos + w // 2, 0, seq) - jnp.clip(pos - w // 2, 0, seq)
            diff = total / count.astype(jnp.float32) - halo_ref[base:base + ts, lanes]
            diffs.append(diff.astype(jnp.bfloat16))
        return diffs

    def pool_mix(diffs):
        pooled = []
        for g, diff in enumerate(diffs):
            lanes = slice(g * group_dim, (g + 1) * group_dim)
            y = jnp.dot(diff, wpool_ref[g], preferred_element_type=jnp.float32)
            pooled.append((y + bpool_ref[:, lanes]) * pscale_ref[:, lanes])
        return jnp.concatenate(pooled, axis=1).astype(jnp.bfloat16)

    def pool_project(pm):
        return jnp.dot(pm, wop_ref[...], preferred_element_type=jnp.float32)

    def output_project(half_z, attn_out, pool_out):
        gates2 = jnp.tanh(half_z) + 1.0
        mix2 = gates2[:, :d] * attn_out + gates2[:, d:] * pool_out
        return jnp.dot(mix2.astype(jnp.bfloat16), wout_ref[...], preferred_element_type=jnp.float32) + bout_ref[...]

    def finish(r0, x, y):
        x1 = _layer_norm(alpha * x + y, lng_ref[...], lnb_ref[...])
        _store_rows(x1_ref, x1, r0)
        x_hi = x1.astype(jnp.bfloat16)
        x_lo = (x1 - x_hi.astype(jnp.float32)).astype(jnp.bfloat16)
        nt = (((1,), (1,)), ((), ()))
        both = lax.dot_general(wr_ref[...], x_hi, nt, preferred_element_type=jnp.float32)
        cross = lax.dot_general(wr_ref[0:N_EXPERTS, :], x_lo, nt, preferred_element_type=jnp.float32)
        logits_t = both[0:N_EXPERTS, :] + both[N_EXPERTS:, :] + cross + br_ref[...]
        cls_ref[0, :, r0:r0 + ts] = _route_class(logits_t)

    starts = [n * ts for n in range(MIX_SUBTILES)]
    diffs = [window_diffs(r0) for r0 in starts]
    staged = [projections(r0) for r0 in starts]
    pooled = [pool_mix(df) for df in diffs]
    pool_outs = [pool_project(pm) for pm in pooled]
    ys = [output_project(half_z, attn_out, po) for (_, half_z, attn_out), po in zip(staged, pool_outs)]
    for r0, (x, _, _), y in zip(starts, staged, ys):
        finish(r0, x, y)

    onehot = lax.broadcasted_iota(jnp.int32, (CLASS_ROWS, tm), 0) == cls_ref[0]
    cnt_ref[0] = jnp.sum(onehot.astype(jnp.float32), axis=1, keepdims=True).astype(jnp.int32)


def _mix_call(x, a, p, lw, seq, alpha):
    attn_dim, t = a.shape
    tm = TOK_TILE
    assert seq % tm == 0 and tm % POOL_HALO == 0
    pool_dim = p.shape[1]
    row = lambda i: (i, 0)
    halo_blocks = t // POOL_HALO
    per_tile = tm // POOL_HALO
    prev = lambda i: (jnp.maximum(i * per_tile - 1, 0), 0)
    nxt = lambda i: (jnp.minimum((i + 1) * per_tile, halo_blocks - 1), 0)
    weights = [lw["w_gate"], lw["b_gate"], lw["w_pool"], lw["b_pool"], lw["pool_scale"], lw["w_oa"],
               lw["w_op"], lw["w_out"], lw["b_out"], lw["ln1_g"], lw["ln1_b"], lw["w_router_t"],
               lw["b_router_t"]]
    return pl.pallas_call(
        functools.partial(_mix_kernel, seq=seq, alpha=alpha),
        grid=(t // tm,),
        in_specs=[_tok_spec(tm), pl.BlockSpec((attn_dim, tm), lambda i: (0, i)),
                  pl.BlockSpec((tm, pool_dim), row), pl.BlockSpec((POOL_HALO, pool_dim), prev),
                  pl.BlockSpec((POOL_HALO, pool_dim), nxt)] + [_const_spec(w.shape) for w in weights],
        out_specs=[_tok_spec(tm), pl.BlockSpec((1, 1, tm), lambda i: (i, 0, 0)),
                   pl.BlockSpec((1, CLASS_ROWS, 1), lambda i: (i, 0, 0))],
        out_shape=[jax.ShapeDtypeStruct((t * ROW_CHUNKS, LANES), jnp.float32),
                   jax.ShapeDtypeStruct((t // tm, 1, tm), jnp.int32),
                   jax.ShapeDtypeStruct((t // tm, CLASS_ROWS, 1), jnp.int32)],
        scratch_shapes=[pltpu.VMEM((tm + 2 * POOL_HALO, pool_dim), jnp.float32)],
        compiler_params=_params(("parallel",)),
        name="mix",
    )(x, a, p, p, p, *weights)


def _rank_kernel(cls_ref, start_ref, dest_ref, carry_ref):
    tm = cls_ref.shape[2]

    @pl.when(pl.program_id(0) == 0)
    def _():
        carry_ref[...] = start_ref[...]

    cls = cls_ref[0]
    onehot = lax.broadcasted_iota(jnp.int32, (CLASS_ROWS, tm), 0) == cls
    upper = (lax.broadcasted_iota(jnp.int32, (tm, tm), 0)
             <= lax.broadcasted_iota(jnp.int32, (tm, tm), 1)).astype(jnp.bfloat16)
    incl = jnp.dot(onehot.astype(jnp.bfloat16), upper, preferred_element_type=jnp.float32)
    slot = incl - 1.0 + carry_ref[...]
    dest_ref[0] = jnp.sum(jnp.where(onehot, slot, 0.0), axis=0, keepdims=True).astype(jnp.int32)
    carry_ref[...] = carry_ref[...] + incl[:, tm - 1:tm]


def _rank_call(cls, class_start):
    nt, _, tm = cls.shape
    return pl.pallas_call(
        _rank_kernel,
        grid=(nt,),
        in_specs=[pl.BlockSpec((1, 1, tm), lambda i: (i, 0, 0)), _const_spec((CLASS_ROWS, 1))],
        out_specs=pl.BlockSpec((1, 1, tm), lambda i: (i, 0, 0)),
        out_shape=jax.ShapeDtypeStruct((nt, 1, tm), jnp.int32),
        scratch_shapes=[pltpu.VMEM((CLASS_ROWS, 1), jnp.float32)],
        compiler_params=_params(("arbitrary",)),
        name="rank",
    )(cls, class_start)


def _slot_table_kernel(pad_lo_ref, pad_hi_ref, dest_ref, table_ref):
    i = pl.program_id(0)
    chunk = dest_ref.shape[2]

    @pl.when(i == 0)
    def _():
        def zero(s, c):
            table_ref[s] = 0
            return c
        for cls in range(N_CLASSES + 1):
            lax.fori_loop(pad_lo_ref[cls], pad_hi_ref[cls], zero, 0)

    def place(r, c):
        table_ref[dest_ref[0, 0, r]] = i * chunk + r
        return c
    lax.fori_loop(0, chunk, place, 0, unroll=8)


def _slot_table_call(dest, pad_lo, pad_hi, n_slots):
    n_chunks, _, chunk = dest.shape
    grid_spec = pltpu.PrefetchScalarGridSpec(
        num_scalar_prefetch=2,
        grid=(n_chunks,),
        in_specs=[pl.BlockSpec((1, 1, chunk), lambda i, lo, hi: (i, 0, 0), memory_space=pltpu.SMEM)],
        out_specs=pl.BlockSpec(memory_space=pltpu.SMEM),
    )
    return pl.pallas_call(
        _slot_table_kernel,
        grid_spec=grid_spec,
        out_shape=jax.ShapeDtypeStruct((n_slots,), jnp.int32),
        compiler_params=_params(("arbitrary",)),
        name="slot_table",
    )(pad_lo, pad_hi, dest)


def _ffn_kernel(ea_ref, eb_ref, nvalid_ref, tok_ref, tok_next_ref, tok_prev_ref, x_hbm,
                w1a_ref, b1a_ref, w2a_ref, b2a_ref, w1b_ref, b1b_ref, w2b_ref, b2b_ref,
                wr_ref, br_ref, lng_ref, lnb_ref, out_hbm, xbuf, obuf, sem_in, sem_out,
                *, alpha, n_tok):
    blk = pl.program_id(0)
    last = pl.num_programs(0) - 1
    slot = blk % 2
    other = 1 - slot
    n_valid = nvalid_ref[blk]
    n_valid_prev = jnp.where(blk > 0, nvalid_ref[jnp.maximum(blk - 1, 0)], 0)
    rows = MOE_BLOCK
    tile = ROW_CHUNKS

    def gather_row(ids_ref, dst_slot, r):
        src = pl.multiple_of(ids_ref[0, 0, r] * tile, tile)
        return pltpu.make_async_copy(x_hbm.at[pl.ds(src, tile)], xbuf.at[dst_slot, pl.ds(r * tile, tile)],
                                     sem_in.at[dst_slot])

    def scatter_row(ids_ref, count, src_slot, r):
        token = jnp.where(r < count, ids_ref[0, 0, r], n_tok + r)
        dst = pl.multiple_of(token * tile, tile)
        return pltpu.make_async_copy(obuf.at[src_slot, pl.ds(r * tile, tile)], out_hbm.at[pl.ds(dst, tile)],
                                     sem_out.at[src_slot])

    def wait_gather(dst_slot):
        pltpu.make_async_copy(x_hbm.at[pl.ds(0, rows * tile)], xbuf.at[dst_slot], sem_in.at[dst_slot]).wait()

    def wait_scatter(src_slot):
        pltpu.make_async_copy(obuf.at[src_slot], out_hbm.at[pl.ds(0, rows * tile)], sem_out.at[src_slot]).wait()

    def start_all(make_copy):
        def pair(j, c):
            make_copy(2 * j).start(priority=0)
            make_copy(2 * j + 1).start(priority=1)
            return c
        lax.fori_loop(0, rows // 2, pair, 0, unroll=4)

    @pl.when(blk == 0)
    def _():
        obuf[1] = jnp.zeros(obuf.shape[1:], obuf.dtype)
        start_all(functools.partial(gather_row, tok_ref, 0))

    wait_gather(slot)

    start_all(functools.partial(gather_row, tok_next_ref, other))
    start_all(functools.partial(scatter_row, tok_prev_ref, n_valid_prev, other))

    @pl.when(n_valid > 0)
    def _():
        x = _load_rows(xbuf.at[slot], rows)
        xb = x.astype(jnp.bfloat16)

        def up(w1_ref, b1_ref):
            return jnp.dot(xb, w1_ref[0], preferred_element_type=jnp.float32) + b1_ref[0]

        def down(h, w2_ref, b2_ref):
            h = 0.5 * h * (1.0 + lax.erf(h * (2.0 ** -0.5)))
            return jnp.dot(h.astype(jnp.bfloat16), w2_ref[0], preferred_element_type=jnp.float32) + b2_ref[0]

        h_a = up(w1a_ref, b1a_ref)
        h_b = up(w1b_ref, b1b_ref)

        ea = ea_ref[blk]
        eb = eb_ref[blk]
        w_diff = wr_ref[pl.ds(eb, 1), :] - wr_ref[pl.ds(ea, 1), :]
        delta = jnp.sum(x * w_diff, axis=-1, keepdims=True) + (br_ref[eb] - br_ref[ea])
        gate_a = 1.0 / (1.0 + jnp.exp(delta))
        gate_b = 1.0 / (1.0 + jnp.exp(-delta))
        y = gate_a * down(h_a, w2a_ref, b2a_ref) + gate_b * down(h_b, w2b_ref, b2b_ref)
        _store_rows(obuf.at[slot], _layer_norm(alpha * x + y, lng_ref[...], lnb_ref[...]))

    wait_scatter(other)

    @pl.when(blk == last)
    def _():
        wait_gather(other)
        start_all(functools.partial(scatter_row, tok_ref, n_valid, slot))
        wait_scatter(slot)


def _ffn_call(x1, t, buf_tok, block_ea, block_eb, block_nvalid, lw, alpha):
    n_blocks = block_ea.shape[0]
    _, d, d_ff = lw["w1"].shape
    const = lambda shape: pl.BlockSpec(shape, lambda b, ea, eb, nv: (0,) * len(shape))
    by_a = lambda shape: pl.BlockSpec(shape, lambda b, ea, eb, nv: (ea[b], 0, 0))
    by_b = lambda shape: pl.BlockSpec(shape, lambda b, ea, eb, nv: (eb[b], 0, 0))
    ids = lambda shift: pl.BlockSpec(
        (1, 1, MOE_BLOCK), lambda b, ea, eb, nv: (jnp.clip(b + shift, 0, n_blocks - 1), 0, 0),
        memory_space=pltpu.SMEM)
    grid_spec = pltpu.PrefetchScalarGridSpec(
        num_scalar_prefetch=3,
        grid=(n_blocks,),
        in_specs=[
            ids(0), ids(1), ids(-1),
            pl.BlockSpec(memory_space=pl.ANY),
            by_a((1, d, d_ff)), by_a((1, 1, d_ff)), by_a((1, d_ff, d)), by_a((1, 1, d)),
            by_b((1, d, d_ff)), by_b((1, 1, d_ff)), by_b((1, d_ff, d)), by_b((1, 1, d)),
            const((N_EXPERTS, d)),
            pl.BlockSpec(memory_space=pltpu.SMEM),
            const((1, d)), const((1, d)),
        ],
        out_specs=pl.BlockSpec(memory_space=pl.ANY),
        scratch_shapes=[pltpu.VMEM((2, MOE_BLOCK * ROW_CHUNKS, LANES), jnp.float32),
                        pltpu.VMEM((2, MOE_BLOCK * ROW_CHUNKS, LANES), jnp.float32),
                        pltpu.SemaphoreType.DMA((2,)), pltpu.SemaphoreType.DMA((2,))],
    )
    return pl.pallas_call(
        functools.partial(_ffn_kernel, alpha=alpha, n_tok=t),
        grid_spec=grid_spec,
        out_shape=jax.ShapeDtypeStruct(((t + MOE_BLOCK) * ROW_CHUNKS, LANES), jnp.float32),
        compiler_params=_params(("arbitrary",)),
        name="ffn",
    )(block_ea, block_eb, block_nvalid, buf_tok, buf_tok, buf_tok, x1,
      lw["w1"], lw["b1"], lw["w2"], lw["b2"], lw["w1"], lw["b1"], lw["w2"], lw["b2"],
      lw["w_router_rows"], lw["b_router"], lw["ln2_g"], lw["ln2_b"])


def _dispatch_tables(cls, tile_counts, t):
    counts = jnp.sum(tile_counts, axis=0)[:N_CLASSES, 0]
    n_blocks = -(-(t + N_CLASSES * (MOE_BLOCK - 1)) // MOE_BLOCK)
    padded = (counts + MOE_BLOCK - 1) // MOE_BLOCK * MOE_BLOCK
    pad_ends = jnp.cumsum(padded)
    pad_starts = pad_ends - padded
    class_start = jnp.zeros((CLASS_ROWS, 1), jnp.float32).at[:N_CLASSES, 0].set(pad_starts.astype(jnp.float32))
    dest = _rank_call(cls, class_start)
    n_slots = n_blocks * MOE_BLOCK
    pad_lo = jnp.concatenate([pad_starts + counts, pad_ends[-1:]]).astype(jnp.int32)
    pad_hi = jnp.concatenate([pad_ends, jnp.full((1,), n_slots, pad_ends.dtype)]).astype(jnp.int32)
    buf_tok = _slot_table_call(dest, pad_lo, pad_hi, n_slots)
    block_start = jnp.arange(n_blocks, dtype=jnp.int32) * MOE_BLOCK
    block_cls = jnp.minimum(jnp.sum(pad_ends[None, :] <= block_start[:, None], axis=1), N_CLASSES - 1)
    block_cls = block_cls.astype(jnp.int32)
    n_valid = jnp.clip(counts[block_cls] - (block_start - pad_starts[block_cls]), 0, MOE_BLOCK)
    pair = jnp.asarray(PAIRS, jnp.int32)[block_cls % len(PAIRS)]
    group = block_cls // len(PAIRS)
    ea = group * EXPERTS_PER_GROUP + pair[:, 0]
    eb = group * EXPERTS_PER_GROUP + pair[:, 1]
    return (buf_tok.reshape(n_blocks, 1, MOE_BLOCK), ea.astype(jnp.int32), eb.astype(jnp.int32),
            n_valid.astype(jnp.int32))


def kernel(x_prompt, x_sample, ln_in_g, ln_in_b, w_in, b_in, rpb, w_pool, b_pool, pool_scale, w_oa, w_op,
           w_out, b_out, ln1_g, ln1_b, w_router, b_router, w1, b1, w2, b2, ln2_g, ln2_b):
    depth, d, _ = w_in.shape
    attn_dim = N_HEADS * HEAD_DIM
    pool_dim = w_op.shape[1]
    qkvp = 3 * attn_dim + pool_dim
    alpha = (2 * depth) ** 0.25
    bf = jnp.bfloat16
    f32 = jnp.float32

    wr_hi = w_router.T.astype(bf)
    wr_lo = (w_router.T - wr_hi.astype(f32)).astype(bf)
    shared = {
        "w_router_t": jnp.concatenate([wr_hi, wr_lo], axis=0),
        "b_router_t": b_router.reshape(N_EXPERTS, 1).astype(f32),
        "w_router_rows": w_router.T.astype(f32),
        "b_router": b_router.astype(f32),
    }
    layers = []
    for l in range(depth):
        lw = dict(shared)
        lw.update({
            "w_qkp": jnp.concatenate([w_in[l, :, :2 * attn_dim], w_in[l, :, 3 * attn_dim:qkvp]],
                                     axis=1).astype(bf),
            "b_qkp": jnp.concatenate([b_in[l, :2 * attn_dim], b_in[l, 3 * attn_dim:qkvp]]).reshape(1, -1),
            "w_v_t": w_in[l, :, 2 * attn_dim:3 * attn_dim].T.astype(bf),
            "b_v_t": b_in[l, 2 * attn_dim:3 * attn_dim].reshape(-1, 1),
            "w_gate": (0.5 * w_in[l, :, qkvp:]).astype(bf), "b_gate": 0.5 * b_in[l, qkvp:].reshape(1, -1),
            "table": _bias_table(rpb[l]),
            "w_pool": w_pool[l].astype(bf), "b_pool": b_pool[l].reshape(1, -1),
            "pool_scale": pool_scale[l].reshape(1, -1),
            "w_oa": w_oa[l].astype(bf), "w_op": w_op[l].astype(bf), "w_out": (0.5 * w_out[l]).astype(bf),
            "b_out": b_out[l].reshape(1, -1),
            "ln1_g": ln1_g[l].reshape(1, -1), "ln1_b": ln1_b[l].reshape(1, -1),
            "w1": w1[l].astype(bf), "b1": b1[l].reshape(N_EXPERTS, 1, -1),
            "w2": w2[l].astype(bf), "b2": b2[l].reshape(N_EXPERTS, 1, -1),
            "ln2_g": ln2_g[l].reshape(1, -1), "ln2_b": ln2_b[l].reshape(1, -1),
        })
        layers.append(lw)

    def run(x3):
        batch, seq, _ = x3.shape
        t = batch * seq
        x = _ln_call(x3.reshape(t, d), ln_in_g.reshape(1, -1), ln_in_b.reshape(1, -1))
        for lw in layers:
            q, k, vt, p = _proj_call(x, t, lw, attn_dim, pool_dim)
            a = _attn_call(q, k, vt, lw["table"], batch, seq)
            x1, cls, tile_counts = _mix_call(x, a, p, lw, seq, alpha)
            buf_tok, ea, eb, n_valid = _dispatch_tables(cls, tile_counts, t)
            x = _ffn_call(x1, t, buf_tok, ea, eb, n_valid, lw, alpha)
        return _unpack_call(x, t).reshape(batch, seq, d)

    return (run(x_prompt), run(x_sample))
```
